```python
import math, functools
import jax, jax.numpy as jnp
from jax import lax
import numpy as np

D_MODEL = 1024
BATCH = 1
SEQ = 16384
DEPTH = 1
DEC_BATCH = 32
DEC_SEQ = 4
PAST_LEN = 16384
PAGE_SIZE = 128

M_HEADS = 4
M_HEAD_DIM = D_MODEL // M_HEADS
M_WIDTH = M_HEADS * M_HEAD_DIM
M_CHUNK = 128
A_HEADS = 8
A_HEAD_DIM = D_MODEL // (2 * A_HEADS)
A_VDIM = 2 * A_HEAD_DIM
A_WIDTH = A_HEADS * A_VDIM
Q_BLOCK = 128
ROPE_THETA = 10000.0
N_KEYS = 128
N_EXPERTS = N_KEYS * N_KEYS
PEER_HEADS = 8
PEER_TOPK = 16
PEER_DKEY = 256
PEER_BLOCK = 128
NORM_EPS = 1e-6
IN_SPLITS = (M_WIDTH, M_WIDTH, M_WIDTH, M_WIDTH, M_HEADS, M_HEADS, A_WIDTH, A_WIDTH, A_WIDTH, D_MODEL, D_MODEL)
IN_COLS = sum(IN_SPLITS)

kernel_name = 'hybrid_mlstm_diffattn_peer_step'


def rmsnorm(x, g):
    xf = x.astype(jnp.float32)
    y = xf * lax.rsqrt(jnp.mean(xf * xf, axis=-1, keepdims=True) + NORM_EPS)
    return y.astype(x.dtype) * g


def rope(x, pos):
    d = x.shape[-1]
    inv = ROPE_THETA ** (-jnp.arange(0, d, 2, dtype=jnp.float32) / d)
    ang = pos.astype(jnp.float32)[:, None] * inv[None, :]
    cos = jnp.cos(ang)[:, None, :]
    sin = jnp.sin(ang)[:, None, :]
    xf = x.astype(jnp.float32)
    x1, x2 = xf[..., : d // 2], xf[..., d // 2:]
    return jnp.concatenate([x1 * cos - x2 * sin, x2 * cos + x1 * sin], axis=-1).astype(x.dtype)


def split_cols(z):
    out, start = [], 0
    for w in IN_SPLITS:
        out.append(z[..., start:start + w])
        start += w
    return out


def mlstm_chunkwise(q, k, v, ig, lf, C0, n0, m0, chunk):
    f32 = jnp.float32
    B, T, H, dh = q.shape
    nc = T // chunk

    def to_chunks(a):
        return jnp.moveaxis(a.reshape((B, nc, chunk) + a.shape[2:]), 1, 0)

    causal = jnp.tril(jnp.ones((chunk, chunk), dtype=bool))

    def step(carry, inp):
        C, n, m = carry
        qc, kc, vc, igc, lfc = inp
        qc, kc, vc = qc.astype(f32), kc.astype(f32), vc.astype(f32)
        b = jnp.cumsum(lfc, axis=1).transpose(0, 2, 1)
        igt = igc.transpose(0, 2, 1)
        D = b[..., :, None] - b[..., None, :] + igt[..., None, :]
        D = jnp.where(causal, D, -jnp.inf)
        m_inter = b + m[..., None]
        m_t = jnp.maximum(m_inter, jnp.max(D, axis=-1))
        w_inter = jnp.exp(m_inter - m_t)
        a = jnp.exp(D - m_t[..., None]) * jnp.einsum('blhd,bshd->bhls', qc, kc)
        num = w_inter[..., None] * jnp.einsum('bhvd,blhd->bhlv', C, qc) + jnp.einsum('bhls,bshv->bhlv', a, vc)
        den = w_inter * jnp.einsum('bhd,blhd->bhl', n, qc) + jnp.sum(a, axis=-1)
        h = num / jnp.maximum(jnp.abs(den), jnp.exp(-m_t))[..., None]
        bL = b[..., -1]
        g = bL[..., None] - b + igt
        m_new = jnp.maximum(bL + m, jnp.max(g, axis=-1))
        dec = jnp.exp(bL + m - m_new)
        w = jnp.exp(g - m_new[..., None])
        C_new = dec[..., None, None] * C + jnp.einsum('bhs,bshv,bshd->bhvd', w, vc, kc)
        n_new = dec[..., None] * n + jnp.einsum('bhs,bshd->bhd', w, kc)
        return (C_new, n_new, m_new), h.transpose(0, 2, 1, 3)

    carry0 = (C0.astype(f32), n0.astype(f32), m0.astype(f32))
    (C, n, m), hs = lax.scan(step, carry0, (to_chunks(q), to_chunks(k), to_chunks(v), to_chunks(ig), to_chunks(lf)))
    h = jnp.moveaxis(hs, 0, 1).reshape(B, T, H, dh)
    return h, C, n, m


def diff_attn_prompt(q1, q2, kk, vv, lam):
    B, S, H, d = q1.shape
    qb = math.gcd(S, Q_BLOCK)
    nb = S // qb
    scale = d ** -0.5
    k1, k2 = kk[..., :d], kk[..., d:]
    kpos = jnp.arange(S)
    q1b = jnp.moveaxis(q1.reshape(B, nb, qb, H, d), 1, 0)
    q2b = jnp.moveaxis(q2.reshape(B, nb, qb, H, d), 1, 0)

    def block(args):
        i, a1, a2 = args
        qpos = i * qb + jnp.arange(qb)
        mask = kpos[None, :] <= qpos[:, None]

        def smap(qq, kx):
            s = jnp.einsum('bqhd,bkhd->bhqk', qq, kx).astype(jnp.float32) * scale
            return jax.nn.softmax(jnp.where(mask, s, -jnp.inf), axis=-1)

        p = smap(a1, k1) - lam * smap(a2, k2)
        return jnp.einsum('bhqk,bkhv->bqhv', p, vv.astype(jnp.float32))

    o = lax.map(block, (jnp.arange(nb), q1b, q2b))
    return jnp.moveaxis(o, 0, 1).reshape(B, S, H, vv.shape[-1]).astype(vv.dtype)


def online_update(state, s, vals):
    m, l, acc = state
    m_new = jnp.maximum(m, jnp.max(s, axis=-1))
    corr = jnp.exp(m - m_new)
    p = jnp.exp(s - m_new[..., None])
    acc = corr[..., None] * acc + jnp.einsum('bhqk,bkhv->bhqv', p, vals.astype(jnp.float32))
    return (m_new, corr * l + jnp.sum(p, axis=-1), acc)


def diff_attn_sample(q1, q2, kk, vv, lam, cache_k, cache_v, page_table, layer):
    Bd, T, H, d = q1.shape
    scale = d ** -0.5
    f32 = jnp.float32

    def scores(keys):
        s1 = jnp.einsum('bqhd,bkhd->bhqk', q1, keys[..., :d]).astype(f32) * scale
        s2 = jnp.einsum('bqhd,bkhd->bhqk', q2, keys[..., d:]).astype(f32) * scale
        return s1, s2

    init = (jnp.full((Bd, H, T), -jnp.inf, f32), jnp.zeros((Bd, H, T), f32), jnp.zeros((Bd, H, T, vv.shape[-1]), f32))

    def page_step(carry, pages):
        keys = cache_k[layer, pages]
        vals = cache_v[layer, pages]
        s1, s2 = scores(keys)
        return (online_update(carry[0], s1, vals), online_update(carry[1], s2, vals)), None

    (st1, st2), _ = lax.scan(page_step, (init, init), page_table.T)
    mask = jnp.tril(jnp.ones((T, T), dtype=bool))
    s1, s2 = scores(kk)
    st1 = online_update(st1, jnp.where(mask, s1, -jnp.inf), vv)
    st2 = online_update(st2, jnp.where(mask, s2, -jnp.inf), vv)
    o = st1[2] / st1[1][..., None] - lam * (st2[2] / st2[1][..., None])
    return o.transpose(0, 2, 1, 3).astype(vv.dtype)


def peer_ffn(h, wq, sub_k1, sub_k2, u, v):
    f32 = jnp.float32
    N = h.shape[0]
    blk = math.gcd(N, PEER_BLOCK)

    def block(xb):
        q = (xb @ wq).reshape(blk, PEER_HEADS, 2, PEER_DKEY // 2).astype(f32)
        s1 = jnp.einsum('thd,hkd->thk', q[:, :, 0], sub_k1.astype(f32))
        s2 = jnp.einsum('thd,hkd->thk', q[:, :, 1], sub_k2.astype(f32))
        t1, i1 = lax.top_k(s1, PEER_TOPK)
        t2, i2 = lax.top_k(s2, PEER_TOPK)
        cand = (t1[..., :, None] + t2[..., None, :]).reshape(blk, PEER_HEADS, PEER_TOPK * PEER_TOPK)
        cid = (i1[..., :, None] * N_KEYS + i2[..., None, :]).reshape(blk, PEER_HEADS, PEER_TOPK * PEER_TOPK)
        best, sel = lax.top_k(cand, PEER_TOPK)
        eid = jnp.take_along_axis(cid, sel, axis=-1)
        gate = jax.nn.softmax(best, axis=-1)
        act = jax.nn.gelu(jnp.einsum('td,thkd->thk', xb, u[eid]).astype(f32), approximate=False)
        return jnp.einsum('thk,thkd->td', (gate * act).astype(xb.dtype), v[eid])

    return lax.map(block, h.reshape(N // blk, blk, D_MODEL)).reshape(N, D_MODEL)


def hybrid_layer(x, c, pos, mlstm_state, attend, chunk, layer_idx,
                 w_ada, b_ada, g_pre1, g_post1, g_pre2, g_post2, w_in, b_if, mlstm_norm,
                 lambda_q1, lambda_k1, lambda_q2, lambda_k2, diff_norm, w_out,
                 peer_wq, peer_k1, peer_k2, peer_u, peer_v):
    f32 = jnp.float32
    B, T, _ = x.shape
    mod = jax.nn.silu(c) @ w_ada + b_ada
    shift1, scale1, gate1, shift2, scale2, gate2 = jnp.split(mod[:, None, :], 6, axis=-1)
    h = rmsnorm(x, g_pre1) * (1 + scale1) + shift1
    mq, mk, mv, mo, mi, mf, dq, dk, dv, ga, gb = split_cols(h @ w_in)
    q = mq.reshape(B, T, M_HEADS, M_HEAD_DIM)
    k = mk.reshape(B, T, M_HEADS, M_HEAD_DIM) * (M_HEAD_DIM ** -0.5)
    v = mv.reshape(B, T, M_HEADS, M_HEAD_DIM)
    ig = (mi + b_if[:M_HEADS]).astype(f32)
    lf = jax.nn.log_sigmoid((mf + b_if[M_HEADS:]).astype(f32))
    hm, C, n, m = mlstm_chunkwise(q, k, v, ig, lf, mlstm_state[0], mlstm_state[1], mlstm_state[2], chunk)
    y_a = rmsnorm(hm.astype(x.dtype), mlstm_norm.reshape(M_HEADS, M_HEAD_DIM)).reshape(B, T, M_WIDTH) * jax.nn.sigmoid(mo)
    qd = dq.reshape(B, T, A_HEADS, 2, A_HEAD_DIM)
    kd = dk.reshape(B, T, A_HEADS, 2, A_HEAD_DIM)
    q1 = rope(qd[..., 0, :], pos)
    q2 = rope(qd[..., 1, :], pos)
    kk = jnp.concatenate([rope(kd[..., 0, :], pos), rope(kd[..., 1, :], pos)], axis=-1)
    vv = dv.reshape(B, T, A_HEADS, A_VDIM)
    lambda_init = 0.8 - 0.6 * math.exp(-0.3 * layer_idx)
    lam = (jnp.exp(jnp.sum(lambda_q1.astype(f32) * lambda_k1.astype(f32)))
           - jnp.exp(jnp.sum(lambda_q2.astype(f32) * lambda_k2.astype(f32))) + lambda_init)
    o = attend(q1, q2, kk, vv, lam)
    y_b = (rmsnorm(o, diff_norm) * (1 - lambda_init)).reshape(B, T, A_WIDTH)
    merged = jax.nn.sigmoid(ga) * y_a + jax.nn.sigmoid(gb) * y_b
    x = x + gate1 * rmsnorm(merged @ w_out, g_post1)
    h2 = rmsnorm(x, g_pre2) * (1 + scale2) + shift2
    f = peer_ffn(h2.reshape(B * T, D_MODEL), peer_wq, peer_k1, peer_k2, peer_u, peer_v).reshape(B, T, D_MODEL)
    x = x + gate2 * rmsnorm(f, g_post2)
    return x, kk, vv, C, n, m


def setup_inputs(seed: int = 0) -> dict:
    key = jax.random.key(seed)
    ks = jax.random.split(key, 40)
    f32 = jnp.float32
    n_pages = PAST_LEN // PAGE_SIZE
    n_pool = (5 * DEC_BATCH * n_pages) // 4

    def nrm(k, shape, s=1.0):
        return jax.random.normal(k, shape, f32) * s

    x_prompt = nrm(ks[0], (BATCH, SEQ, D_MODEL))
    x_sample = nrm(ks[1], (DEC_BATCH, DEC_SEQ, D_MODEL))
    cache_k = nrm(ks[2], (DEPTH, n_pool, PAGE_SIZE, A_HEADS, A_VDIM))
    cache_v = nrm(ks[3], (DEPTH, n_pool, PAGE_SIZE, A_HEADS, A_VDIM))
    state_C = nrm(ks[4], (DEPTH, DEC_BATCH, M_HEADS, M_HEAD_DIM, M_HEAD_DIM), M_HEAD_DIM ** -0.5)
    state_n = nrm(ks[5], (DEPTH, DEC_BATCH, M_HEADS, M_HEAD_DIM), 0.5)
    state_m = nrm(ks[6], (DEPTH, DEC_BATCH, M_HEADS), 0.5)
    page_table = jax.random.permutation(ks[7], n_pool)[: DEC_BATCH * n_pages].reshape(DEC_BATCH, n_pages).astype(jnp.int32)
    c_prompt = nrm(ks[8], (BATCH, D_MODEL))
    c_sample = nrm(ks[9], (DEC_BATCH, D_MODEL))
    w_ada = nrm(ks[10], (DEPTH, D_MODEL, 6 * D_MODEL), 0.3 * D_MODEL ** -0.5)
    b_ada = nrm(ks[11], (DEPTH, 6 * D_MODEL), 0.01)
    g_pre1 = 1.0 + nrm(ks[12], (DEPTH, D_MODEL), 0.05)
    g_post1 = 1.0 + nrm(ks[13], (DEPTH, D_MODEL), 0.05)
    g_pre2 = 1.0 + nrm(ks[14], (DEPTH, D_MODEL), 0.05)
    g_post2 = 1.0 + nrm(ks[15], (DEPTH, D_MODEL), 0.05)
    w_in = nrm(ks[16], (DEPTH, D_MODEL, IN_COLS), D_MODEL ** -0.5)
    b_if = jnp.concatenate([nrm(ks[17], (DEPTH, M_HEADS), 0.1), 3.0 + nrm(ks[18], (DEPTH, M_HEADS), 0.5)], axis=-1)
    mlstm_norm = 1.0 + nrm(ks[19], (DEPTH, M_WIDTH), 0.05)
    lambda_q1 = nrm(ks[20], (DEPTH, A_HEAD_DIM), 0.1)
    lambda_k1 = nrm(ks[21], (DEPTH, A_HEAD_DIM), 0.1)
    lambda_q2 = nrm(ks[22], (DEPTH, A_HEAD_DIM), 0.1)
    lambda_k2 = nrm(ks[23], (DEPTH, A_HEAD_DIM), 0.1)
    diff_norm = 1.0 + nrm(ks[24], (DEPTH, A_VDIM), 0.05)
    w_out = nrm(ks[25], (DEPTH, D_MODEL, D_MODEL), D_MODEL ** -0.5)
    peer_wq = nrm(ks[26], (DEPTH, D_MODEL, PEER_HEADS * PEER_DKEY), D_MODEL ** -0.5)
    peer_k1 = nrm(ks[27], (DEPTH, PEER_HEADS, N_KEYS, PEER_DKEY // 2), (PEER_DKEY // 2) ** -0.5)
    peer_k2 = nrm(ks[28], (DEPTH, PEER_HEADS, N_KEYS, PEER_DKEY // 2), (PEER_DKEY // 2) ** -0.5)
    peer_u = nrm(ks[29], (DEPTH, N_EXPERTS, D_MODEL), D_MODEL ** -0.5)
    peer_v = nrm(ks[30], (DEPTH, N_EXPERTS, D_MODEL), PEER_HEADS ** -0.5)
    return {'x_prompt': x_prompt, 'x_sample': x_sample, 'cache_k': cache_k, 'cache_v': cache_v,
            'state_C': state_C, 'state_n': state_n, 'state_m': state_m, 'page_table': page_table,
            'c_prompt': c_prompt, 'c_sample': c_sample, 'w_ada': w_ada, 'b_ada': b_ada,
            'g_pre1': g_pre1, 'g_post1': g_post1, 'g_pre2': g_pre2, 'g_post2': g_post2,
            'w_in': w_in, 'b_if': b_if, 'mlstm_norm': mlstm_norm, 'lambda_q1': lambda_q1,
            'lambda_k1': lambda_k1, 'lambda_q2': lambda_q2, 'lambda_k2': lambda_k2, 'diff_norm': diff_norm,
            'w_out': w_out, 'peer_wq': peer_wq, 'peer_k1': peer_k1, 'peer_k2': peer_k2,
            'peer_u': peer_u, 'peer_v': peer_v}


def reference(x_prompt, x_sample, cache_k, cache_v, state_C, state_n, state_m, page_table, c_prompt, c_sample,
              w_ada, b_ada, g_pre1, g_post1, g_pre2, g_post2, w_in, b_if, mlstm_norm,
              lambda_q1, lambda_k1, lambda_q2, lambda_k2, diff_norm, w_out,
              peer_wq, peer_k1, peer_k2, peer_u, peer_v):
    f32 = jnp.float32
    pos_prompt = jnp.arange(SEQ, dtype=jnp.int32)
    pos_sample = PAST_LEN + jnp.arange(DEC_SEQ, dtype=jnp.int32)
    zero_state = (jnp.zeros((BATCH, M_HEADS, M_HEAD_DIM, M_HEAD_DIM), f32),
                  jnp.zeros((BATCH, M_HEADS, M_HEAD_DIM), f32),
                  jnp.zeros((BATCH, M_HEADS), f32))
    xp, xs = x_prompt, x_sample
    kp_l, vp_l, Cp_l, np_l, mp_l = [], [], [], [], []
    ks_l, vs_l, Cs_l, ns_l, ms_l = [], [], [], [], []
    for l in range(DEPTH):
        lw = (w_ada[l], b_ada[l], g_pre1[l], g_post1[l], g_pre2[l], g_post2[l], w_in[l], b_if[l], mlstm_norm[l],
              lambda_q1[l], lambda_k1[l], lambda_q2[l], lambda_k2[l], diff_norm[l], w_out[l],
              peer_wq[l], peer_k1[l], peer_k2[l], peer_u[l], peer_v[l])
        xp, kp, vp, Cp, n_p, mp = hybrid_layer(xp, c_prompt, pos_prompt, zero_state, diff_attn_prompt,
                                               math.gcd(SEQ, M_CHUNK), l, *lw)
        attend_s = functools.partial(diff_attn_sample, cache_k=cache_k, cache_v=cache_v,
                                     page_table=page_table, layer=l)
        xs, k_s, v_s, Cs, n_s, ms = hybrid_layer(xs, c_sample, pos_sample, (state_C[l], state_n[l], state_m[l]),
                                                 attend_s, DEC_SEQ, l, *lw)
        kp_l.append(kp); vp_l.append(vp); Cp_l.append(Cp); np_l.append(n_p); mp_l.append(mp)
        ks_l.append(k_s); vs_l.append(v_s); Cs_l.append(Cs); ns_l.append(n_s); ms_l.append(ms)
    return (xp, xs, jnp.stack(kp_l), jnp.stack(vp_l), jnp.stack(Cp_l), jnp.stack(np_l), jnp.stack(mp_l),
            jnp.stack(ks_l), jnp.stack(vs_l), jnp.stack(Cs_l), jnp.stack(ns_l), jnp.stack(ms_l))
```

```python
import functools
import math

import jax
import jax.numpy as jnp
from jax import lax
from jax.experimental import pallas as pl
from jax.experimental.pallas import tpu as pltpu

F32 = jnp.float32
BF16 = jnp.bfloat16

D_MODEL = 1024
M_HEADS = 4
M_HEAD_DIM = 256
A_HEADS = 8
A_HEAD_DIM = 64
A_VDIM = 128
ROPE_THETA = 10000.0
N_KEYS = 128
PEER_HEADS = 8
PEER_TOPK = 16
NORM_EPS = 1e-6
LAMBDA_INIT = 0.8 - 0.6 * math.exp(-0.3 * 0)
N_MAIN_COLS = 9 * D_MODEL

LANES = 128
NEG = -1e30
VMEM_LIMIT = 56 * 1024 * 1024


def _cparams(sem):
    return pltpu.CompilerParams(dimension_semantics=sem, vmem_limit_bytes=VMEM_LIMIT)


def _nt(a, b):
    return lax.dot_general(a, b, (((1,), (1,)), ((), ())), preferred_element_type=F32)


def _tn(a, b):
    return lax.dot_general(a, b, (((0,), (0,)), ((), ())), preferred_element_type=F32)


def _rms(x, g):
    return x * lax.rsqrt(jnp.mean(x * x, axis=-1, keepdims=True) + NORM_EPS) * g


def _split3(x):
    x1 = x.astype(BF16)
    r1 = x - x1.astype(F32)
    x2 = r1.astype(BF16)
    x3 = (r1 - x2.astype(F32)).astype(BF16)
    return x1, x2, x3


def _ada_kernel(c_ref, w_ref, b_ref, o_ref):
    c = c_ref[...]
    s = c * jax.nn.sigmoid(c)
    o_ref[...] = jnp.dot(s.astype(BF16), w_ref[...].astype(BF16), preferred_element_type=F32) + b_ref[...]


def _ada(c_all, w_ada, b_ada):
    r = c_all.shape[0]
    n = w_ada.shape[1]
    bn = D_MODEL
    return pl.pallas_call(
        _ada_kernel,
        grid=(n // bn,),
        in_specs=[pl.BlockSpec((r, D_MODEL), lambda j: (0, 0)),
                  pl.BlockSpec((D_MODEL, bn), lambda j: (0, j)),
                  pl.BlockSpec((1, bn), lambda j: (0, j))],
        out_specs=pl.BlockSpec((r, bn), lambda j: (0, j)),
        out_shape=jax.ShapeDtypeStruct((r, n), F32),
        compiler_params=_cparams(("arbitrary",)),
        name="ada_mod",
    )(c_all, w_ada, b_ada.reshape(1, n))


def _rope128(x, cos, sin_signed):
    fwd = pltpu.roll(x, LANES - 32, axis=1)
    bwd = pltpu.roll(x, 32, axis=1)
    lane = lax.broadcasted_iota(jnp.int32, x.shape, 1)
    rot = jnp.where((lane % 64) < 32, fwd, bwd)
    return x * cos + rot * sin_signed


def _inproj_kernel(x_ref, sc_ref, sh_ref, g_ref, wm_ref, wif_ref, bif_ref, inv_ref,
                   qkv_ref, mo_ref, gab_ref, gates_ref, q1_ref, q2_ref, kf_ref, kb_ref, vf_ref, vb_ref,
                   *, tb, pos0, period):
    x = x_ref[...]
    h = _rms(x, g_ref[...]) * (1.0 + sc_ref[...]) + sh_ref[...]
    hb = h.astype(BF16)

    def proj(i):
        return jnp.dot(hb, wm_ref[:, i * D_MODEL:(i + 1) * D_MODEL], preferred_element_type=F32)

    qkv_ref[:, 0:D_MODEL] = proj(0).astype(BF16)
    qkv_ref[:, D_MODEL:2 * D_MODEL] = (proj(1) * (M_HEAD_DIM ** -0.5)).astype(BF16)
    qkv_ref[:, 2 * D_MODEL:3 * D_MODEL] = proj(2).astype(BF16)
    mo_ref[...] = proj(3)
    gab_ref[:, 0:D_MODEL] = proj(7)
    gab_ref[:, D_MODEL:2 * D_MODEL] = proj(8)

    zif = jnp.dot(hb, wif_ref[...], preferred_element_type=F32) + bif_ref[...]
    lane = lax.broadcasted_iota(jnp.int32, zif.shape, 1)
    logsig = jnp.minimum(zif, 0.0) - jnp.log1p(jnp.exp(-jnp.abs(zif)))
    gates_ref[...] = jnp.where(lane < M_HEADS, zif, logsig)

    row = lax.broadcasted_iota(jnp.int32, (tb, LANES), 0) + pl.program_id(0) * tb
    pos = (pos0 + row % period).astype(F32)
    ang = pos * inv_ref[...]
    cos = jnp.cos(ang)
    sin = jnp.sin(ang)
    lane = lax.broadcasted_iota(jnp.int32, (tb, LANES), 1)
    sin_signed = jnp.where((lane % 64) < 32, -sin, sin)
    first = lane < A_HEAD_DIM

    dq = proj(4)
    dk = proj(5)
    dv = proj(6)
    vf_ref[...] = dv
    vb_ref[...] = dv.astype(BF16)
    for hd in range(A_HEADS):
        sl = slice(hd * LANES, (hd + 1) * LANES)
        qr = _rope128(dq[:, sl], cos, sin_signed) * (A_HEAD_DIM ** -0.5)
        q1_ref[:, sl] = jnp.where(first, qr, 0.0).astype(BF16)
        q2_ref[:, sl] = jnp.where(first, 0.0, qr).astype(BF16)
        kr = _rope128(dk[:, sl], cos, sin_signed)
        kf_ref[:, sl] = kr
        kb_ref[:, sl] = kr.astype(BF16)


def _inproj(x, scale, shift, g_pre, w_main, w_if, b_if128, inv128, *, pos0, period, tb):
    t = x.shape[0]
    mod_rows = scale.shape[0]
    if mod_rows == 1:
        mod_spec = pl.BlockSpec((1, D_MODEL), lambda i: (0, 0))
    else:
        mod_spec = pl.BlockSpec((tb, D_MODEL), lambda i: (i, 0))
    const = lambda shape: pl.BlockSpec(shape, lambda i: (0, 0))
    tok = lambda w: pl.BlockSpec((tb, w), lambda i: (i, 0))
    outs = [
        (3 * D_MODEL, BF16),
        (D_MODEL, F32),
        (2 * D_MODEL, F32),
        (LANES, F32),
        (D_MODEL, BF16),
        (D_MODEL, BF16),
        (D_MODEL, F32),
        (D_MODEL, BF16),
        (D_MODEL, F32),
        (D_MODEL, BF16),
    ]
    return pl.pallas_call(
        functools.partial(_inproj_kernel, tb=tb, pos0=pos0, period=period),
        grid=(t // tb,),
        in_specs=[tok(D_MODEL), mod_spec, mod_spec, const((1, D_MODEL)),
                  const((D_MODEL, N_MAIN_COLS)), const((D_MODEL, LANES)), const((1, LANES)), const((1, LANES))],
        out_specs=[tok(w) for w, _ in outs],
        out_shape=[jax.ShapeDtypeStruct((t, w), dt) for w, dt in outs],
        compiler_params=_cparams(("parallel",)),
        name="in_proj",
    )(x, scale, shift, g_pre, w_main, w_if, b_if128, inv128)


def _mlstm_kernel(qkv_ref, gates_ref, mo_ref, nw_ref, c0_ref, n0_ref, m0_ref,
                  ya_ref, c_ref, n_ref, m_ref, *, L):
    @pl.when(pl.program_id(1) == 0)
    def _():
        c_ref[...] = c0_ref[...]
        n_ref[...] = n0_ref[...]
        m_ref[...] = m0_ref[...]

    gates = gates_ref[...]
    r = lax.broadcasted_iota(jnp.int32, (L, L), 0)
    c = lax.broadcasted_iota(jnp.int32, (L, L), 1)
    causal = c <= r
    tril = jnp.where(causal, 1.0, 0.0).astype(BF16)
    eye8 = jnp.where(lax.broadcasted_iota(jnp.int32, (8, LANES), 0)
                     == lax.broadcasted_iota(jnp.int32, (8, LANES), 1), 1.0, 0.0).astype(BF16)
    g3 = _split3(gates)
    cums = sum(jnp.dot(tril, gi, preferred_element_type=F32) for gi in g3)
    gt8 = sum(_nt(eye8, gi) for gi in g3)
    bt8 = sum(_nt(eye8, ci) for ci in _split3(cums))

    for h in range(M_HEADS):
        sl = slice(h * M_HEAD_DIM, (h + 1) * M_HEAD_DIM)
        q = qkv_ref[:, h * M_HEAD_DIM:(h + 1) * M_HEAD_DIM]
        k = qkv_ref[:, D_MODEL + h * M_HEAD_DIM:D_MODEL + (h + 1) * M_HEAD_DIM]
        v = qkv_ref[:, 2 * D_MODEL + h * M_HEAD_DIM:2 * D_MODEL + (h + 1) * M_HEAD_DIM]
        ig_col = gates[:, h:h + 1]
        ig_row = gt8[h:h + 1, :]
        b_col = cums[:, M_HEADS + h:M_HEADS + h + 1]
        b_row = bt8[M_HEADS + h:M_HEADS + h + 1, :]
        m_prev = m_ref[0, h:h + 1, 0:1]
        c_old = c_ref[0, h]
        n_old = n_ref[0, h:h + 1, :]

        dmat = jnp.where(causal, b_col - b_row + ig_row, NEG)
        m_inter = b_col + m_prev
        m_t = jnp.maximum(m_inter, jnp.max(dmat, axis=-1, keepdims=True))
        w_inter = jnp.exp(m_inter - m_t)
        a = jnp.exp(dmat - m_t) * _nt(q, k)
        num = w_inter * _nt(q, c_old.astype(BF16)) + jnp.dot(a.astype(BF16), v, preferred_element_type=F32)
        qn = jnp.sum(q.astype(F32) * n_old, axis=-1, keepdims=True)
        den = w_inter * qn + jnp.sum(a, axis=-1, keepdims=True)
        hm = num / jnp.maximum(jnp.abs(den), jnp.exp(-m_t))

        b_last = b_col[L - 1:L, :]
        g_row = b_last - b_row + ig_row
        g_col = b_last - b_col + ig_col
        m_new = jnp.maximum(b_last + m_prev, jnp.max(g_row, axis=-1, keepdims=True))
        dec = jnp.exp(b_last + m_prev - m_new)
        w_col = jnp.exp(g_col - m_new)
        vw = (v.astype(F32) * w_col).astype(BF16)
        c_ref[0, h] = dec * c_old + _tn(vw, k)
        n_ref[0, h:h + 1, :] = dec * n_old + jnp.sum(k.astype(F32) * w_col, axis=0, keepdims=True)
        m_ref[0, h:h + 1, :] = jnp.broadcast_to(m_new, (1, LANES))

        ya_ref[:, sl] = _rms(hm, nw_ref[:, sl]) * jax.nn.sigmoid(mo_ref[:, sl])


def _mlstm(qkv, gates, mo, norm_w, c0, n0, m0, *, batch, L):
    t_total = qkv.shape[0]
    nc = t_total // (batch * L)
    tok = lambda w: pl.BlockSpec((L, w), lambda b, c: (b * nc + c, 0))
    st4 = pl.BlockSpec((1, M_HEADS, M_HEAD_DIM, M_HEAD_DIM), lambda b, c: (b, 0, 0, 0))
    st3n = pl.BlockSpec((1, M_HEADS, M_HEAD_DIM), lambda b, c: (b, 0, 0))
    st3m = pl.BlockSpec((1, M_HEADS, LANES), lambda b, c: (b, 0, 0))
    return pl.pallas_call(
        functools.partial(_mlstm_kernel, L=L),
        grid=(batch, nc),
        in_specs=[tok(3 * D_MODEL), tok(LANES), tok(D_MODEL), pl.BlockSpec((1, D_MODEL), lambda b, c: (0, 0)),
                  st4, st3n, st3m],
        out_specs=[tok(D_MODEL), st4, st3n, st3m],
        out_shape=[jax.ShapeDtypeStruct((t_total, D_MODEL), F32),
                   jax.ShapeDtypeStruct(c0.shape, F32),
                   jax.ShapeDtypeStruct(n0.shape, F32),
                   jax.ShapeDtypeStruct(m0.shape, F32)],
        compiler_params=_cparams(("parallel", "arbitrary")),
        name="mlstm",
    )(qkv, gates, mo, norm_w, c0, n0, m0)


def _lambda_full(lq1_ref, lk1_ref, lq2_ref, lk2_ref):
    a = jnp.exp(jnp.sum(lq1_ref[...] * lk1_ref[...], axis=-1, keepdims=True))
    b = jnp.exp(jnp.sum(lq2_ref[...] * lk2_ref[...], axis=-1, keepdims=True))
    return a - b + LAMBDA_INIT


def _attn_prompt_kernel(q1_ref, q2_ref, k_ref, v_ref, lq1_ref, lk1_ref, lq2_ref, lk2_ref, dn_ref,
                        o_ref, m_sc, l_sc, acc_sc, *, tq):
    qi = pl.program_id(0)
    ki = pl.program_id(1)

    @pl.when(ki == 0)
    def _():
        m_sc[...] = jnp.full(m_sc.shape, NEG, F32)
        l_sc[...] = jnp.zeros(l_sc.shape, F32)
        acc_sc[...] = jnp.zeros(acc_sc.shape, F32)

    def step(masked):
        if masked:
            r = lax.broadcasted_iota(jnp.int32, (tq, tq), 0)
            c = lax.broadcasted_iota(jnp.int32, (tq, tq), 1)
            keep = c <= r
        for h in range(A_HEADS):
            sl = slice(h * LANES, (h + 1) * LANES)
            kh = k_ref[:, sl]
            vh = v_ref[:, sl]
            for mp, q_ref in enumerate((q1_ref, q2_ref)):
                idx = 2 * h + mp
                s = _nt(q_ref[:, sl], kh)
                if masked:
                    s = jnp.where(keep, s, NEG)
                m_prev = m_sc[idx]
                m_new = jnp.maximum(m_prev, jnp.max(s, axis=-1, keepdims=True))
                alpha = jnp.exp(m_prev - m_new)
                p = jnp.exp(s - m_new)
                l_sc[idx] = alpha * l_sc[idx] + jnp.sum(p, axis=-1, keepdims=True)
                acc_sc[idx] = alpha * acc_sc[idx] + jnp.dot(p.astype(BF16), vh, preferred_element_type=F32)
                m_sc[idx] = m_new

    @pl.when(ki < qi)
    def _():
        step(False)

    @pl.when(ki == qi)
    def _():
        step(True)
        lam = _lambda_full(lq1_ref, lk1_ref, lq2_ref, lk2_ref)
        for h in range(A_HEADS):
            o = acc_sc[2 * h] / l_sc[2 * h] - lam * (acc_sc[2 * h + 1] / l_sc[2 * h + 1])
            o_ref[:, h * LANES:(h + 1) * LANES] = _rms(o, dn_ref[...]) * (1.0 - LAMBDA_INIT)


def _attn_prompt(q1, q2, kb, vb, lams, diff_norm, *, tq):
    t = q1.shape[0]
    nq = t // tq
    qspec = pl.BlockSpec((tq, D_MODEL), lambda i, j: (i, 0))
    kvspec = pl.BlockSpec((tq, D_MODEL), lambda i, j: (jnp.minimum(i, j), 0))
    small = lambda n: pl.BlockSpec((1, n), lambda i, j: (0, 0))
    return pl.pallas_call(
        functools.partial(_attn_prompt_kernel, tq=tq),
        grid=(nq, nq),
        in_specs=[qspec, qspec, kvspec, kvspec] + [small(A_HEAD_DIM)] * 4 + [small(A_VDIM)],
        out_specs=pl.BlockSpec((tq, D_MODEL), lambda i, j: (i, 0)),
        out_shape=jax.ShapeDtypeStruct((t, D_MODEL), F32),
        scratch_shapes=[pltpu.VMEM((2 * A_HEADS, tq, 1), F32),
                        pltpu.VMEM((2 * A_HEADS, tq, 1), F32),
                        pltpu.VMEM((2 * A_HEADS, tq, A_VDIM), F32)],
        compiler_params=_cparams(("parallel", "arbitrary")),
        name="diff_attn_prompt",
    )(q1, q2, kb, vb, *lams, diff_norm)


def _attn_sample_kernel(pt_ref, q_ref, kn_ref, vn_ref, lq1_ref, lk1_ref, lq2_ref, lk2_ref, dn_ref, *rest,
                        ppb, n_new, page):
    k_refs = rest[:ppb]
    v_refs = rest[ppb:2 * ppb]
    o_ref, m_sc, l_sc, acc_sc = rest[2 * ppb:]
    j = pl.program_id(1)
    rows = q_ref.shape[1]

    @pl.when(j == 0)
    def _():
        m_sc[...] = jnp.full(m_sc.shape, NEG, F32)
        l_sc[...] = jnp.zeros(l_sc.shape, F32)
        acc_sc[...] = jnp.zeros(acc_sc.shape, F32)

    def update(h, s, vh):
        m_prev = m_sc[h]
        m_new = jnp.maximum(m_prev, jnp.max(s, axis=-1, keepdims=True))
        alpha = jnp.exp(m_prev - m_new)
        p = jnp.exp(s - m_new)
        l_sc[h] = alpha * l_sc[h] + jnp.sum(p, axis=-1, keepdims=True)
        acc_sc[h] = alpha * acc_sc[h] + jnp.dot(p.astype(BF16), vh, preferred_element_type=F32)
        m_sc[h] = m_new

    for h in range(A_HEADS):
        sl = slice(h * LANES, (h + 1) * LANES)
        qh = q_ref[0, :, sl]
        for pg in range(ppb):
            kh = k_refs[pg][0, :, sl].astype(BF16)
            vh = v_refs[pg][0, :, sl].astype(BF16)
            update(h, _nt(qh, kh), vh)

    @pl.when(j == pl.num_programs(1) - 1)
    def _():
        lam = _lambda_full(lq1_ref, lk1_ref, lq2_ref, lk2_ref)
        r = lax.broadcasted_iota(jnp.int32, (rows, rows), 0)
        c = lax.broadcasted_iota(jnp.int32, (rows, rows), 1)
        keep = (c <= r % n_new) & (c < n_new)
        for h in range(A_HEADS):
            sl = slice(h * LANES, (h + 1) * LANES)
            s = jnp.where(keep, _nt(q_ref[0, :, sl], kn_ref[0, :, sl]), NEG)
            update(h, s, vn_ref[0, :, sl])
            o = acc_sc[h] / l_sc[h]
            od = o[0:n_new, :] - lam * o[n_new:2 * n_new, :]
            o_ref[0, :, sl] = _rms(od, dn_ref[...]) * (1.0 - LAMBDA_INIT)


def _attn_sample(q_rows, k_new, v_new, cache_k, cache_v, page_table, lams, diff_norm, *, ppb, n_new):
    bsz, n_pages = page_table.shape
    page = cache_k.shape[1]
    rows = q_rows.shape[1]
    seq = lambda: pl.BlockSpec((1, rows, D_MODEL), lambda b, j, pt: (b, 0, 0))
    small = lambda n: pl.BlockSpec((1, n), lambda b, j, pt: (0, 0))

    def page_spec(i):
        return pl.BlockSpec((1, page, D_MODEL), lambda b, j, pt: (pt[b, j * ppb + i], 0, 0))

    grid_spec = pltpu.PrefetchScalarGridSpec(
        num_scalar_prefetch=1,
        grid=(bsz, n_pages // ppb),
        in_specs=[seq(), seq(), seq()] + [small(A_HEAD_DIM)] * 4 + [small(A_VDIM)]
                 + [page_spec(i) for i in range(ppb)] + [page_spec(i) for i in range(ppb)],
        out_specs=pl.BlockSpec((1, n_new, D_MODEL), lambda b, j, pt: (b, 0, 0)),
        scratch_shapes=[pltpu.VMEM((A_HEADS, rows, 1), F32),
                        pltpu.VMEM((A_HEADS, rows, 1), F32),
                        pltpu.VMEM((A_HEADS, rows, A_VDIM), F32)],
    )
    return pl.pallas_call(
        functools.partial(_attn_sample_kernel, ppb=ppb, n_new=n_new, page=page),
        grid_spec=grid_spec,
        out_shape=jax.ShapeDtypeStruct((bsz, n_new, D_MODEL), F32),
        compiler_params=_cparams(("parallel", "arbitrary")),
        name="diff_attn_sample",
    )(page_table, q_rows, k_new, v_new, *lams, diff_norm, *([cache_k] * ppb), *([cache_v] * ppb))


def _postmix_kernel(x_ref, ya_ref, yb_ref, gab_ref, g1_ref, sc2_ref, sh2_ref, gpost_ref, gpre_ref, wo_ref, wq_ref,
                    x1_ref, h2_ref, qp_ref):
    merged = (jax.nn.sigmoid(gab_ref[:, 0:D_MODEL]) * ya_ref[...]
              + jax.nn.sigmoid(gab_ref[:, D_MODEL:2 * D_MODEL]) * yb_ref[...])
    o = jnp.dot(merged.astype(BF16), wo_ref[...], preferred_element_type=F32)
    x1 = x_ref[...] + g1_ref[...] * _rms(o, gpost_ref[...])
    x1_ref[...] = x1
    h2 = (_rms(x1, gpre_ref[...]) * (1.0 + sc2_ref[...]) + sh2_ref[...]).astype(BF16)
    h2_ref[...] = h2
    qp_ref[...] = jnp.dot(h2, wq_ref[...], preferred_element_type=F32).astype(BF16)


def _postmix(x, ya, yb, gab, gate1, scale2, shift2, g_post1, g_pre2, w_out, w_q, *, tb):
    t = x.shape[0]
    nq = w_q.shape[1]
    if gate1.shape[0] == 1:
        mod_spec = pl.BlockSpec((1, D_MODEL), lambda i: (0, 0))
    else:
        mod_spec = pl.BlockSpec((tb, D_MODEL), lambda i: (i, 0))
    const = lambda shape: pl.BlockSpec(shape, lambda i: (0, 0))
    tok = lambda w: pl.BlockSpec((tb, w), lambda i: (i, 0))
    return pl.pallas_call(
        _postmix_kernel,
        grid=(t // tb,),
        in_specs=[tok(D_MODEL), tok(D_MODEL), tok(D_MODEL), tok(2 * D_MODEL), mod_spec, mod_spec, mod_spec,
                  const((1, D_MODEL)), const((1, D_MODEL)), const((D_MODEL, D_MODEL)), const((D_MODEL, nq))],
        out_specs=[tok(D_MODEL), tok(D_MODEL), tok(nq)],
        out_shape=[jax.ShapeDtypeStruct((t, D_MODEL), F32),
                   jax.ShapeDtypeStruct((t, D_MODEL), BF16),
                   jax.ShapeDtypeStruct((t, nq), BF16)],
        compiler_params=_cparams(("parallel",)),
        name="post_mix",
    )(x, ya, yb, gab, gate1, scale2, shift2, g_post1, g_pre2, w_out, w_q)


def _top_values(s, n):
    out = []
    work = s
    for _ in range(n):
        mx = jnp.max(work, axis=0, keepdims=True)
        out.append(mx)
        work = jnp.where(work == mx, NEG, work)
    return out


def _peer_score_kernel(qp_ref, k1_ref, k2_ref, s1_ref, e1_ref, s2_ref, e2_ref, tau_ref, *, tt):
    ab = N_KEYS // 8
    for h in range(PEER_HEADS):
        base = h * 2 * N_KEYS
        s1 = _nt(k1_ref[h], qp_ref[:, base:base + N_KEYS])
        s2 = _nt(k2_ref[h], qp_ref[:, base + N_KEYS:base + 2 * N_KEYS])
        t1 = _top_values(s1, PEER_TOPK)
        t2 = _top_values(s2, PEER_TOPK)
        t2s = jnp.concatenate(t2, axis=0)
        cand = jnp.concatenate([t1[i] + t2s for i in range(PEER_TOPK)], axis=0)
        tau = _top_values(cand, PEER_TOPK)[-1]
        top = t1[0] + t2[0]
        z = jnp.sum(jnp.where(cand >= tau, jnp.exp(cand - top), 0.0), axis=0, keepdims=True)
        s1_ref[:, h] = s1.reshape(ab, 8, tt)
        e1_ref[:, h] = (jnp.exp(s1 - t1[0]) / z).reshape(ab, 8, tt)
        s2_ref[h] = s2
        e2_ref[h] = jnp.exp(s2 - t2[0])
        tau_ref[h:h + 1, :] = tau


def _peer_scores(qp, k1, k2, *, tt):
    t = qp.shape[0]
    ab = N_KEYS // 8
    a_spec = pl.BlockSpec((ab, PEER_HEADS, 8, tt), lambda i: (0, 0, 0, i))
    b_spec = pl.BlockSpec((PEER_HEADS, N_KEYS, tt), lambda i: (0, 0, i))
    kspec = pl.BlockSpec((PEER_HEADS, N_KEYS, N_KEYS), lambda i: (0, 0, 0))
    a_shape = jax.ShapeDtypeStruct((ab, PEER_HEADS, 8, t), F32)
    b_shape = jax.ShapeDtypeStruct((PEER_HEADS, N_KEYS, t), F32)
    return pl.pallas_call(
        functools.partial(_peer_score_kernel, tt=tt),
        grid=(t // tt,),
        in_specs=[pl.BlockSpec((tt, qp.shape[1]), lambda i: (i, 0)), kspec, kspec],
        out_specs=[a_spec, a_spec, b_spec, b_spec, pl.BlockSpec((PEER_HEADS, tt), lambda i: (0, i))],
        out_shape=[a_shape, a_shape, b_shape, b_shape, jax.ShapeDtypeStruct((PEER_HEADS, t), F32)],
        compiler_params=_cparams(("parallel",)),
        name="peer_scores",
    )(qp, k1, k2)


def _gelu(x):
    return 0.5 * x * (1.0 + lax.erf(x * (2.0 ** -0.5)))


def _peer_dense_kernel(h2_ref, u_ref, vt_ref, s1_ref, e1_ref, s2_ref, e2_ref, tau_ref, x1_ref, g2_ref, gpost_ref,
                       y_ref, acc_sc, a_sc, *, a_per):
    j = pl.program_id(1)

    @pl.when(j == 0)
    def _():
        acc_sc[...] = jnp.zeros(acc_sc.shape, F32)

    zt = _nt(u_ref[...], h2_ref[...])
    for al in range(a_per):
        w = None
        for h in range(PEER_HEADS):
            s1a = s1_ref[al // 8, h, al % 8:al % 8 + 1, :]
            e1a = e1_ref[al // 8, h, al % 8:al % 8 + 1, :]
            sel = (s1a + s2_ref[h]) >= tau_ref[h:h + 1, :]
            term = jnp.where(sel, e1a * e2_ref[h], 0.0)
            w = term if w is None else w + term
        rows = slice(al * N_KEYS, (al + 1) * N_KEYS)
        a_sc[rows, :] = (w * _gelu(zt[rows, :])).astype(BF16)
    acc_sc[...] += jnp.dot(vt_ref[...], a_sc[...], preferred_element_type=F32)

    @pl.when(j == pl.num_programs(1) - 1)
    def _():
        f = acc_sc[...].T
        y_ref[...] = x1_ref[...] + g2_ref[...] * _rms(f, gpost_ref[...])


def _peer_dense(h2, u_b, vt_b, s1, e1, s2, e2, tau, x1, gate2, g_post2, *, tt, a_per):
    t = h2.shape[0]
    n_exp = u_b.shape[0]
    eb = a_per * N_KEYS
    if gate2.shape[0] == 1:
        mod_spec = pl.BlockSpec((1, D_MODEL), lambda i, j: (0, 0))
    else:
        mod_spec = pl.BlockSpec((tt, D_MODEL), lambda i, j: (i, 0))
    a_spec = pl.BlockSpec((a_per // 8, PEER_HEADS, 8, tt), lambda i, j: (j, 0, 0, i))
    b_spec = pl.BlockSpec((PEER_HEADS, N_KEYS, tt), lambda i, j: (0, 0, i))
    return pl.pallas_call(
        functools.partial(_peer_dense_kernel, a_per=a_per),
        grid=(t // tt, n_exp // eb),
        in_specs=[pl.BlockSpec((tt, D_MODEL), lambda i, j: (i, 0)),
                  pl.BlockSpec((eb, D_MODEL), lambda i, j: (j, 0)),
                  pl.BlockSpec((D_MODEL, eb), lambda i, j: (0, j)),
                  a_spec, a_spec, b_spec, b_spec,
                  pl.BlockSpec((PEER_HEADS, tt), lambda i, j: (0, i)),
                  pl.BlockSpec((tt, D_MODEL), lambda i, j: (i, 0)),
                  mod_spec,
                  pl.BlockSpec((1, D_MODEL), lambda i, j: (0, 0))],
        out_specs=pl.BlockSpec((tt, D_MODEL), lambda i, j: (i, 0)),
        out_shape=jax.ShapeDtypeStruct((t, D_MODEL), F32),
        scratch_shapes=[pltpu.VMEM((D_MODEL, tt), F32), pltpu.VMEM((eb, tt), BF16)],
        compiler_params=_cparams(("parallel", "arbitrary")),
        name="peer_dense",
    )(h2, u_b, vt_b, s1, e1, s2, e2, tau, x1, gate2, g_post2)


def _block(t, pref):
    return pref if t % pref == 0 else t


def _layer(x, mods, weights, *, pos0, period, mlstm_state, mlstm_batch, mlstm_chunk, attend):
    shift1, scale1, gate1, shift2, scale2, gate2 = mods
    w = weights
    t = x.shape[0]
    tb = _block(t, 256)
    qkv, mo, gab, gates, q1, q2, kf, kb, vf, vb = _inproj(
        x, scale1, shift1, w["g_pre1"], w["w_main"], w["w_if"], w["b_if128"], w["inv128"],
        pos0=pos0, period=period, tb=tb)

    ya, c_new, n_new, m_new = mlstm_state["run"](qkv, gates, mo)
    yb = attend(q1, q2, kf, kb, vf, vb)

    x1, h2, qp = _postmix(x, ya, yb, gab, gate1, scale2, shift2, w["g_post1"], w["g_pre2"], w["w_out"], w["w_q"], tb=tb)
    tt = _block(t, 128)
    s1, e1, s2, e2, tau = _peer_scores(qp, w["k1"], w["k2"], tt=tt)
    ttd = _block(t, 512)
    y = _peer_dense(h2, w["u"], w["vt"], s1, e1, s2, e2, tau, x1, gate2, w["g_post2"], tt=ttd, a_per=8)
    return y, kf, vf, c_new, n_new, m_new


def _hybrid_step(x_prompt, x_sample, cache_k, cache_v, state_C, state_n, state_m, page_table, c_prompt, c_sample,
                 w_ada, b_ada, g_pre1, g_post1, g_pre2, g_post2, w_in, b_if, mlstm_norm,
                 lambda_q1, lambda_k1, lambda_q2, lambda_k2, diff_norm, w_out,
                 peer_wq, peer_k1, peer_k2, peer_u, peer_v, *, m_chunk=128, tq=512, ppb=4):
    l = 0
    bp, sp, _ = x_prompt.shape
    bs, ts, _ = x_sample.shape
    assert bp == 1
    row = lambda a: a.reshape(1, -1)

    wi = w_in[l]
    n_if = 2 * M_HEADS
    w_main = jnp.concatenate([wi[:, :4 * D_MODEL], wi[:, 4 * D_MODEL + n_if:]], axis=1).astype(BF16)
    w_if = jnp.pad(wi[:, 4 * D_MODEL:4 * D_MODEL + n_if], ((0, 0), (0, LANES - n_if))).astype(BF16)
    b_if128 = jnp.pad(b_if[l], (0, LANES - n_if)).reshape(1, LANES)
    inv = ROPE_THETA ** (-jnp.arange(0, A_HEAD_DIM, 2, dtype=F32) / A_HEAD_DIM)
    inv128 = jnp.tile(inv, LANES // inv.shape[0]).reshape(1, LANES)
    weights = dict(
        g_pre1=row(g_pre1[l]), g_post1=row(g_post1[l]), g_pre2=row(g_pre2[l]), g_post2=row(g_post2[l]),
        w_main=w_main, w_if=w_if, b_if128=b_if128, inv128=inv128,
        w_out=w_out[l].astype(BF16), w_q=peer_wq[l].astype(BF16),
        k1=peer_k1[l].astype(BF16), k2=peer_k2[l].astype(BF16),
        u=peer_u[l].astype(BF16), vt=peer_v[l].T.astype(BF16))
    lams = [row(a[l]) for a in (lambda_q1, lambda_k1, lambda_q2, lambda_k2)]
    dn = row(diff_norm[l])
    nw = row(mlstm_norm[l])

    n_c = bp + bs
    c_all = jnp.pad(jnp.concatenate([c_prompt, c_sample], axis=0), ((0, (-n_c) % 8), (0, 0)))
    mod = _ada(c_all, w_ada[l], b_ada[l])
    mods_p = [mod[0:1, i * D_MODEL:(i + 1) * D_MODEL] for i in range(6)]
    mods_s = [jnp.repeat(mod[bp:n_c, i * D_MODEL:(i + 1) * D_MODEL], ts, axis=0) for i in range(6)]

    def run_mlstm_p(qkv, gates, mo):
        c0 = jnp.zeros((bp, M_HEADS, M_HEAD_DIM, M_HEAD_DIM), F32)
        n0 = jnp.zeros((bp, M_HEADS, M_HEAD_DIM), F32)
        m0 = jnp.zeros((bp, M_HEADS, LANES), F32)
        return _mlstm(qkv, gates, mo, nw, c0, n0, m0, batch=bp, L=math.gcd(sp, m_chunk))

    def attend_p(q1, q2, kf, kb, vf, vb):
        return _attn_prompt(q1, q2, kb, vb, lams, dn, tq=_block(sp, tq))

    yp, kp, vp, cp, n_p, mp = _layer(
        x_prompt.reshape(sp, D_MODEL), mods_p, weights, pos0=0, period=sp,
        mlstm_state=dict(run=run_mlstm_p), mlstm_batch=bp, mlstm_chunk=m_chunk, attend=attend_p)

    lpad = 16
    past_len = page_table.shape[1] * cache_k.shape[2]

    def pad_tokens(a, fill_row=None):
        a3 = a.reshape(bs, ts, a.shape[-1])
        if fill_row is None:
            out = jnp.pad(a3, ((0, 0), (0, lpad - ts), (0, 0)))
        else:
            fill = jnp.broadcast_to(fill_row.astype(a.dtype), (bs, lpad - ts, a.shape[-1]))
            out = jnp.concatenate([a3, fill], axis=1)
        return out.reshape(bs * lpad, a.shape[-1])

    def run_mlstm_s(qkv, gates, mo):
        lane = jnp.arange(LANES)
        gate_fill = jnp.where(lane < M_HEADS, NEG, 0.0).astype(F32)
        m0 = jnp.broadcast_to(state_m[l][:, :, None], (bs, M_HEADS, LANES))
        ya, c_new, n_new, m_new = _mlstm(pad_tokens(qkv), pad_tokens(gates, gate_fill), pad_tokens(mo), nw,
                                         state_C[l], state_n[l], m0, batch=bs, L=lpad)
        ya = ya.reshape(bs, lpad, D_MODEL)[:, :ts].reshape(bs * ts, D_MODEL)
        return ya, c_new, n_new, m_new

    def attend_s(q1, q2, kf, kb, vf, vb):
        rows3 = lambda a: a.reshape(bs, ts, D_MODEL)
        zeros = lambda n: jnp.zeros((bs, n, D_MODEL), BF16)
        q_rows = jnp.concatenate([rows3(q1), rows3(q2), zeros(lpad - 2 * ts)], axis=1)
        k_new = jnp.concatenate([rows3(kb), zeros(lpad - ts)], axis=1)
        v_new = jnp.concatenate([rows3(vb), zeros(lpad - ts)], axis=1)
        n_pool, page = cache_k.shape[1], cache_k.shape[2]
        ck = cache_k[l].reshape(n_pool, page, D_MODEL)
        cv = cache_v[l].reshape(n_pool, page, D_MODEL)
        o = _attn_sample(q_rows, k_new, v_new, ck, cv, page_table, lams, dn,
                         ppb=math.gcd(page_table.shape[1], ppb), n_new=ts)
        return o.reshape(bs * ts, D_MODEL)

    ys, k_s, v_s, cs, n_s, ms = _layer(
        x_sample.reshape(bs * ts, D_MODEL), mods_s, weights, pos0=past_len, period=ts,
        mlstm_state=dict(run=run_mlstm_s), mlstm_batch=bs, mlstm_chunk=ts, attend=attend_s)

    return (yp.reshape(bp, sp, D_MODEL), ys.reshape(bs, ts, D_MODEL),
            kp.reshape(1, bp, sp, A_HEADS, A_VDIM), vp.reshape(1, bp, sp, A_HEADS, A_VDIM),
            cp[None], n_p[None], mp[None, :, :, 0],
            k_s.reshape(1, bs, ts, A_HEADS, A_VDIM), v_s.reshape(1, bs, ts, A_HEADS, A_VDIM),
            cs[None], n_s[None], ms[None, :, :, 0])


def kernel(x_prompt, x_sample, cache_k, cache_v, state_C, state_n, state_m, page_table, c_prompt, c_sample, w_ada, b_ada, g_pre1, g_post1, g_pre2, g_post2, w_in, b_if, mlstm_norm, lambda_q1, lambda_k1, lambda_q2, lambda_k2, diff_norm, w_out, peer_wq, peer_k1, peer_k2, peer_u, peer_v):
    return _hybrid_step(x_prompt, x_sample, cache_k, cache_v, state_C, state_n, state_m, page_table, c_prompt, c_sample,
                        w_ada, b_ada, g_pre1, g_post1, g_pre2, g_post2, w_in, b_if, mlstm_norm,
                        lambda_q1, lambda_k1, lambda_q2, lambda_k2, diff_norm, w_out,
                        peer_wq, peer_k1, peer_k2, peer_u, peer_v)
```

```python
import functools
import math

import jax
import jax.numpy as jnp
from jax import lax
from jax.experimental import pallas as pl
from jax.experimental.pallas import tpu as pltpu

F32 = jnp.float32
BF16 = jnp.bfloat16

D_MODEL = 1024
M_HEADS = 4
M_HEAD_DIM = 256
A_HEADS = 8
A_HEAD_DIM = 64
A_VDIM = 128
ROPE_THETA = 10000.0
N_KEYS = 128
PEER_HEADS = 8
PEER_TOPK = 16
NORM_EPS = 1e-6
LAMBDA_INIT = 0.8 - 0.6 * math.exp(-0.3 * 0)
N_MAIN_COLS = 9 * D_MODEL

LANES = 128
NEG = -1e30
VMEM_LIMIT = 56 * 1024 * 1024


def _cparams(sem):
    return pltpu.CompilerParams(dimension_semantics=sem, vmem_limit_bytes=VMEM_LIMIT)


def _nt(a, b):
    return lax.dot_general(a, b, (((1,), (1,)), ((), ())), preferred_element_type=F32)


def _tn(a, b):
    return lax.dot_general(a, b, (((0,), (0,)), ((), ())), preferred_element_type=F32)


def _rms(x, g):
    return x * lax.rsqrt(jnp.mean(x * x, axis=-1, keepdims=True) + NORM_EPS) * g


def _split3(x):
    x1 = x.astype(BF16)
    r1 = x - x1.astype(F32)
    x2 = r1.astype(BF16)
    x3 = (r1 - x2.astype(F32)).astype(BF16)
    return x1, x2, x3


def _ada_kernel(c_ref, w_ref, b_ref, o_ref):
    c = c_ref[...]
    s = c * jax.nn.sigmoid(c)
    o_ref[...] = jnp.dot(s.astype(BF16), w_ref[...].astype(BF16), preferred_element_type=F32) + b_ref[...]


def _ada(c_all, w_ada, b_ada):
    r = c_all.shape[0]
    n = w_ada.shape[1]
    bn = D_MODEL
    return pl.pallas_call(
        _ada_kernel,
        grid=(n // bn,),
        in_specs=[pl.BlockSpec((r, D_MODEL), lambda j: (0, 0)),
                  pl.BlockSpec((D_MODEL, bn), lambda j: (0, j)),
                  pl.BlockSpec((1, bn), lambda j: (0, j))],
        out_specs=pl.BlockSpec((r, bn), lambda j: (0, j)),
        out_shape=jax.ShapeDtypeStruct((r, n), F32),
        compiler_params=_cparams(("arbitrary",)),
        name="ada_mod",
    )(c_all, w_ada, b_ada.reshape(1, n))


def _rope128(x, cos, sin_signed):
    fwd = pltpu.roll(x, LANES - 32, axis=1)
    bwd = pltpu.roll(x, 32, axis=1)
    lane = lax.broadcasted_iota(jnp.int32, x.shape, 1)
    rot = jnp.where((lane % 64) < 32, fwd, bwd)
    return x * cos + rot * sin_signed


def _inproj_kernel(x_ref, sc_ref, sh_ref, g_ref, wm_ref, wif_ref, bif_ref, inv_ref,
                   qkv_ref, mo_ref, gab_ref, gates_ref, q1_ref, q2_ref, kf_ref, kb_ref, vf_ref, vb_ref,
                   *, tb, pos0, period, head_major):
    x = x_ref[...]
    h = _rms(x, g_ref[...]) * (1.0 + sc_ref[...]) + sh_ref[...]
    hb = h.astype(BF16)

    def proj(i):
        return jnp.dot(hb, wm_ref[:, i * D_MODEL:(i + 1) * D_MODEL], preferred_element_type=F32)

    qkv_ref[:, 0:D_MODEL] = proj(0).astype(BF16)
    qkv_ref[:, D_MODEL:2 * D_MODEL] = (proj(1) * (M_HEAD_DIM ** -0.5)).astype(BF16)
    qkv_ref[:, 2 * D_MODEL:3 * D_MODEL] = proj(2).astype(BF16)
    mo_ref[...] = proj(3)
    gab_ref[:, 0:D_MODEL] = proj(7)
    gab_ref[:, D_MODEL:2 * D_MODEL] = proj(8)

    zif = jnp.dot(hb, wif_ref[...], preferred_element_type=F32) + bif_ref[...]
    lane = lax.broadcasted_iota(jnp.int32, zif.shape, 1)
    logsig = jnp.minimum(zif, 0.0) - jnp.log1p(jnp.exp(-jnp.abs(zif)))
    gates_ref[...] = jnp.where(lane < M_HEADS, zif, logsig)

    row = lax.broadcasted_iota(jnp.int32, (tb, LANES), 0) + pl.program_id(0) * tb
    pos = (pos0 + row % period).astype(F32)
    ang = pos * inv_ref[...]
    cos = jnp.cos(ang)
    sin = jnp.sin(ang)
    lane = lax.broadcasted_iota(jnp.int32, (tb, LANES), 1)
    sin_signed = jnp.where((lane % 64) < 32, -sin, sin)
    first = lane < A_HEAD_DIM

    dq = proj(4)
    dk = proj(5)
    dv = proj(6)
    vf_ref[...] = dv
    if not head_major:
        vb_ref[...] = dv.astype(BF16)
    for hd in range(A_HEADS):
        sl = slice(hd * LANES, (hd + 1) * LANES)
        qr = _rope128(dq[:, sl], cos, sin_signed) * (A_HEAD_DIM ** -0.5)
        q1 = jnp.where(first, qr, 0.0)
        q2 = jnp.where(first, 0.0, qr)
        kr = _rope128(dk[:, sl], cos, sin_signed)
        kf_ref[:, sl] = kr
        if head_major:
            q1_ref[hd] = q1.T.astype(BF16)
            q2_ref[hd] = q2.T.astype(BF16)
            kb_ref[hd] = kr.astype(BF16)
            vb_ref[hd] = dv[:, sl].T.astype(BF16)
        else:
            q1_ref[:, sl] = q1.astype(BF16)
            q2_ref[:, sl] = q2.astype(BF16)
            kb_ref[:, sl] = kr.astype(BF16)


def _inproj(x, scale, shift, g_pre, w_main, w_if, b_if128, inv128, *, pos0, period, tb, head_major):
    t = x.shape[0]
    mod_rows = scale.shape[0]
    if mod_rows == 1:
        mod_spec = pl.BlockSpec((1, D_MODEL), lambda i: (0, 0))
    else:
        mod_spec = pl.BlockSpec((tb, D_MODEL), lambda i: (i, 0))
    const = lambda shape: pl.BlockSpec(shape, lambda i: (0, 0))
    tok = lambda w: (pl.BlockSpec((tb, w), lambda i: (i, 0)), (t, w))
    if head_major:
        feat_t = (pl.BlockSpec((A_HEADS, A_VDIM, tb), lambda i: (0, 0, i)), (A_HEADS, A_VDIM, t))
        tok_h = (pl.BlockSpec((A_HEADS, tb, A_VDIM), lambda i: (0, i, 0)), (A_HEADS, t, A_VDIM))
        q_out, k_out, v_out = feat_t, tok_h, feat_t
    else:
        q_out = k_out = v_out = tok(D_MODEL)
    outs = [
        (tok(3 * D_MODEL), BF16),
        (tok(D_MODEL), F32),
        (tok(2 * D_MODEL), F32),
        (tok(LANES), F32),
        (q_out, BF16),
        (q_out, BF16),
        (tok(D_MODEL), F32),
        (k_out, BF16),
        (tok(D_MODEL), F32),
        (v_out, BF16),
    ]
    return pl.pallas_call(
        functools.partial(_inproj_kernel, tb=tb, pos0=pos0, period=period, head_major=head_major),
        grid=(t // tb,),
        in_specs=[tok(D_MODEL)[0], mod_spec, mod_spec, const((1, D_MODEL)),
                  const((D_MODEL, N_MAIN_COLS)), const((D_MODEL, LANES)), const((1, LANES)), const((1, LANES))],
        out_specs=[spec for (spec, _), _ in outs],
        out_shape=[jax.ShapeDtypeStruct(shape, dt) for (_, shape), dt in outs],
        compiler_params=_cparams(("parallel",)),
        name="in_proj",
    )(x, scale, shift, g_pre, w_main, w_if, b_if128, inv128)


def _mlstm_kernel(qkv_ref, gates_ref, mo_ref, nw_ref, c0_ref, n0_ref, m0_ref,
                  ya_ref, c_ref, n_ref, m_ref, *, L):
    @pl.when(pl.program_id(1) == 0)
    def _():
        c_ref[...] = c0_ref[...]
        n_ref[...] = n0_ref[...]
        m_ref[...] = m0_ref[...]

    gates = gates_ref[...]
    r = lax.broadcasted_iota(jnp.int32, (L, L), 0)
    c = lax.broadcasted_iota(jnp.int32, (L, L), 1)
    causal = c <= r
    tril = jnp.where(causal, 1.0, 0.0).astype(BF16)
    eye8 = jnp.where(lax.broadcasted_iota(jnp.int32, (8, LANES), 0)
                     == lax.broadcasted_iota(jnp.int32, (8, LANES), 1), 1.0, 0.0).astype(BF16)
    g3 = _split3(gates)
    cums = sum(jnp.dot(tril, gi, preferred_element_type=F32) for gi in g3)
    gt8 = sum(_nt(eye8, gi) for gi in g3)
    bt8 = sum(_nt(eye8, ci) for ci in _split3(cums))

    for h in range(M_HEADS):
        sl = slice(h * M_HEAD_DIM, (h + 1) * M_HEAD_DIM)
        q = qkv_ref[:, h * M_HEAD_DIM:(h + 1) * M_HEAD_DIM]
        k = qkv_ref[:, D_MODEL + h * M_HEAD_DIM:D_MODEL + (h + 1) * M_HEAD_DIM]
        v = qkv_ref[:, 2 * D_MODEL + h * M_HEAD_DIM:2 * D_MODEL + (h + 1) * M_HEAD_DIM]
        ig_col = gates[:, h:h + 1]
        ig_row = gt8[h:h + 1, :]
        b_col = cums[:, M_HEADS + h:M_HEADS + h + 1]
        b_row = bt8[M_HEADS + h:M_HEADS + h + 1, :]
        m_prev = m_ref[0, h:h + 1, 0:1]
        c_old = c_ref[0, h]
        n_old = n_ref[0, h:h + 1, :]

        dmat = jnp.where(causal, b_col - b_row + ig_row, NEG)
        m_inter = b_col + m_prev
        m_t = jnp.maximum(m_inter, jnp.max(dmat, axis=-1, keepdims=True))
        w_inter = jnp.exp(m_inter - m_t)
        a = jnp.exp(dmat - m_t) * _nt(q, k)
        num = w_inter * _nt(q, c_old.astype(BF16)) + jnp.dot(a.astype(BF16), v, preferred_element_type=F32)
        qn = jnp.sum(q.astype(F32) * n_old, axis=-1, keepdims=True)
        den = w_inter * qn + jnp.sum(a, axis=-1, keepdims=True)
        hm = num / jnp.maximum(jnp.abs(den), jnp.exp(-m_t))

        b_last = b_col[L - 1:L, :]
        g_row = b_last - b_row + ig_row
        g_col = b_last - b_col + ig_col
        m_new = jnp.maximum(b_last + m_prev, jnp.max(g_row, axis=-1, keepdims=True))
        dec = jnp.exp(b_last + m_prev - m_new)
        w_col = jnp.exp(g_col - m_new)
        vw = (v.astype(F32) * w_col).astype(BF16)
        c_ref[0, h] = dec * c_old + _tn(vw, k)
        n_ref[0, h:h + 1, :] = dec * n_old + jnp.sum(k.astype(F32) * w_col, axis=0, keepdims=True)
        m_ref[0, h:h + 1, :] = jnp.broadcast_to(m_new, (1, LANES))

        ya_ref[:, sl] = _rms(hm, nw_ref[:, sl]) * jax.nn.sigmoid(mo_ref[:, sl])


def _mlstm(qkv, gates, mo, norm_w, c0, n0, m0, *, batch, L):
    t_total = qkv.shape[0]
    nc = t_total // (batch * L)
    tok = lambda w: pl.BlockSpec((L, w), lambda b, c: (b * nc + c, 0))
    st4 = pl.BlockSpec((1, M_HEADS, M_HEAD_DIM, M_HEAD_DIM), lambda b, c: (b, 0, 0, 0))
    st3n = pl.BlockSpec((1, M_HEADS, M_HEAD_DIM), lambda b, c: (b, 0, 0))
    st3m = pl.BlockSpec((1, M_HEADS, LANES), lambda b, c: (b, 0, 0))
    return pl.pallas_call(
        functools.partial(_mlstm_kernel, L=L),
        grid=(batch, nc),
        in_specs=[tok(3 * D_MODEL), tok(LANES), tok(D_MODEL), pl.BlockSpec((1, D_MODEL), lambda b, c: (0, 0)),
                  st4, st3n, st3m],
        out_specs=[tok(D_MODEL), st4, st3n, st3m],
        out_shape=[jax.ShapeDtypeStruct((t_total, D_MODEL), F32),
                   jax.ShapeDtypeStruct(c0.shape, F32),
                   jax.ShapeDtypeStruct(n0.shape, F32),
                   jax.ShapeDtypeStruct(m0.shape, F32)],
        compiler_params=_cparams(("parallel", "arbitrary")),
        name="mlstm",
    )(qkv, gates, mo, norm_w, c0, n0, m0)


def _lambda_full(lq1_ref, lk1_ref, lq2_ref, lk2_ref):
    a = jnp.exp(jnp.sum(lq1_ref[...] * lk1_ref[...], axis=-1, keepdims=True))
    b = jnp.exp(jnp.sum(lq2_ref[...] * lk2_ref[...], axis=-1, keepdims=True))
    return a - b + LAMBDA_INIT


def _attn_prompt_kernel(q1_ref, q2_ref, k_ref, vt_ref, lq1_ref, lk1_ref, lq2_ref, lk2_ref, dn_ref,
                        o_ref, m_sc, l_sc, acc_sc, *, tq):
    qi = pl.program_id(0)
    ki = pl.program_id(1)

    @pl.when(ki == 0)
    def _():
        m_sc[...] = jnp.full(m_sc.shape, NEG, F32)
        l_sc[...] = jnp.zeros(l_sc.shape, F32)
        acc_sc[...] = jnp.zeros(acc_sc.shape, F32)

    def step(masked):
        if masked:
            key = lax.broadcasted_iota(jnp.int32, (tq, tq), 0)
            qry = lax.broadcasted_iota(jnp.int32, (tq, tq), 1)
            keep = key <= qry
        for h in range(A_HEADS):
            kh = k_ref[h]
            vth = vt_ref[h]
            for mp, q_ref in enumerate((q1_ref, q2_ref)):
                idx = 2 * h + mp
                s = jnp.dot(kh, q_ref[h], preferred_element_type=F32)
                if masked:
                    s = jnp.where(keep, s, NEG)
                m_prev = m_sc[idx]
                m_new = jnp.maximum(m_prev, jnp.max(s, axis=0, keepdims=True))
                alpha = jnp.exp(m_prev - m_new)
                p = jnp.exp(s - m_new)
                l_sc[idx] = alpha * l_sc[idx] + jnp.sum(p, axis=0, keepdims=True)
                acc_sc[idx] = alpha * acc_sc[idx] + jnp.dot(vth, p.astype(BF16), preferred_element_type=F32)
                m_sc[idx] = m_new

    @pl.when(ki < qi)
    def _():
        step(False)

    @pl.when(ki == qi)
    def _():
        step(True)
        lam = _lambda_full(lq1_ref, lk1_ref, lq2_ref, lk2_ref)
        for h in range(A_HEADS):
            ot = acc_sc[2 * h] / l_sc[2 * h] - lam * (acc_sc[2 * h + 1] / l_sc[2 * h + 1])
            o_ref[:, h * LANES:(h + 1) * LANES] = _rms(ot.T, dn_ref[...]) * (1.0 - LAMBDA_INIT)


def _attn_prompt(q1t, q2t, kh, vt, lams, diff_norm, *, tq):
    t = kh.shape[1]
    nq = t // tq
    qspec = pl.BlockSpec((A_HEADS, A_VDIM, tq), lambda i, j: (0, 0, i))
    kspec = pl.BlockSpec((A_HEADS, tq, A_VDIM), lambda i, j: (0, jnp.minimum(i, j), 0))
    vspec = pl.BlockSpec((A_HEADS, A_VDIM, tq), lambda i, j: (0, 0, jnp.minimum(i, j)))
    small = lambda n: pl.BlockSpec((1, n), lambda i, j: (0, 0))
    return pl.pallas_call(
        functools.partial(_attn_prompt_kernel, tq=tq),
        grid=(nq, nq),
        in_specs=[qspec, qspec, kspec, vspec] + [small(A_HEAD_DIM)] * 4 + [small(A_VDIM)],
        out_specs=pl.BlockSpec((tq, D_MODEL), lambda i, j: (i, 0)),
        out_shape=jax.ShapeDtypeStruct((t, D_MODEL), F32),
        scratch_shapes=[pltpu.VMEM((2 * A_HEADS, 1, tq), F32),
                        pltpu.VMEM((2 * A_HEADS, 1, tq), F32),
                        pltpu.VMEM((2 * A_HEADS, A_VDIM, tq), F32)],
        compiler_params=_cparams(("parallel", "arbitrary")),
        name="diff_attn_prompt",
    )(q1t, q2t, kh, vt, *lams, diff_norm)


def _attn_sample_kernel(pt_ref, q_ref, kn_ref, vn_ref, lq1_ref, lk1_ref, lq2_ref, lk2_ref, dn_ref, *rest,
                        ppb, n_new, page):
    k_refs = rest[:ppb]
    v_refs = rest[ppb:2 * ppb]
    o_ref, m_sc, l_sc, acc_sc = rest[2 * ppb:]
    j = pl.program_id(1)
    rows = q_ref.shape[2]

    @pl.when(j == 0)
    def _():
        m_sc[...] = jnp.full(m_sc.shape, NEG, F32)
        l_sc[...] = jnp.zeros(l_sc.shape, F32)
        acc_sc[...] = jnp.zeros(acc_sc.shape, F32)

    def update(scores, values):
        s = jnp.concatenate([scores(h) for h in range(A_HEADS)], axis=0)
        m_prev = m_sc[...]
        m_new = jnp.maximum(m_prev, jnp.max(s, axis=-1, keepdims=True))
        alpha = jnp.exp(m_prev - m_new)
        p = jnp.exp(s - m_new)
        l_sc[...] = alpha * l_sc[...] + jnp.sum(p, axis=-1, keepdims=True)
        pb = p.astype(BF16)
        pv = jnp.concatenate([jnp.dot(pb[h * rows:(h + 1) * rows], values(h), preferred_element_type=F32)
                              for h in range(A_HEADS)], axis=0)
        acc_sc[...] = alpha * acc_sc[...] + pv
        m_sc[...] = m_new

    def head_rows(refs, h):
        parts = [r[0, pl.ds(h, page, stride=A_HEADS), :] for r in refs]
        return jnp.concatenate(parts, axis=0).astype(BF16)

    update(lambda h: _nt(q_ref[0, h], head_rows(k_refs, h)), lambda h: head_rows(v_refs, h))

    @pl.when(j == pl.num_programs(1) - 1)
    def _():
        lam = _lambda_full(lq1_ref, lk1_ref, lq2_ref, lk2_ref)
        r = lax.broadcasted_iota(jnp.int32, (rows, rows), 0)
        c = lax.broadcasted_iota(jnp.int32, (rows, rows), 1)
        keep = (c <= r % n_new) & (c < n_new)
        update(lambda h: jnp.where(keep, _nt(q_ref[0, h], kn_ref[0, h]), NEG), lambda h: vn_ref[0, h])
        o = acc_sc[...] / l_sc[...]
        for h in range(A_HEADS):
            od = o[h * rows:h * rows + n_new, :] - lam * o[h * rows + n_new:h * rows + 2 * n_new, :]
            o_ref[0, :, h * LANES:(h + 1) * LANES] = _rms(od, dn_ref[...]) * (1.0 - LAMBDA_INIT)


def _attn_sample(q_rows, k_new, v_new, cache_k, cache_v, page_table, lams, diff_norm, *, ppb, n_new, page_base):
    bsz, n_pages = page_table.shape
    page = cache_k.shape[1] // A_HEADS
    rows = q_rows.shape[2]
    seq = lambda: pl.BlockSpec((1, A_HEADS, rows, A_VDIM), lambda b, j, pt: (b, 0, 0, 0))
    small = lambda n: pl.BlockSpec((1, n), lambda b, j, pt: (0, 0))

    def page_spec(i):
        return pl.BlockSpec((1, page * A_HEADS, A_VDIM), lambda b, j, pt: (page_base + pt[b, j * ppb + i], 0, 0))

    grid_spec = pltpu.PrefetchScalarGridSpec(
        num_scalar_prefetch=1,
        grid=(bsz, n_pages // ppb),
        in_specs=[seq(), seq(), seq()] + [small(A_HEAD_DIM)] * 4 + [small(A_VDIM)]
                 + [page_spec(i) for i in range(ppb)] + [page_spec(i) for i in range(ppb)],
        out_specs=pl.BlockSpec((1, n_new, D_MODEL), lambda b, j, pt: (b, 0, 0)),
        scratch_shapes=[pltpu.VMEM((A_HEADS * rows, 1), F32),
                        pltpu.VMEM((A_HEADS * rows, 1), F32),
                        pltpu.VMEM((A_HEADS * rows, A_VDIM), F32)],
    )
    return pl.pallas_call(
        functools.partial(_attn_sample_kernel, ppb=ppb, n_new=n_new, page=page),
        grid_spec=grid_spec,
        out_shape=jax.ShapeDtypeStruct((bsz, n_new, D_MODEL), F32),
        compiler_params=_cparams(("parallel", "arbitrary")),
        name="diff_attn_sample",
    )(page_table, q_rows, k_new, v_new, *lams, diff_norm, *([cache_k] * ppb), *([cache_v] * ppb))


def _postmix_kernel(x_ref, ya_ref, yb_ref, gab_ref, g1_ref, sc2_ref, sh2_ref, gpost_ref, gpre_ref, wo_ref, wq_ref,
                    x1_ref, h2_ref, qp_ref):
    merged = (jax.nn.sigmoid(gab_ref[:, 0:D_MODEL]) * ya_ref[...]
              + jax.nn.sigmoid(gab_ref[:, D_MODEL:2 * D_MODEL]) * yb_ref[...])
    o = jnp.dot(merged.astype(BF16), wo_ref[...], preferred_element_type=F32)
    x1 = x_ref[...] + g1_ref[...] * _rms(o, gpost_ref[...])
    x1_ref[...] = x1
    h2 = (_rms(x1, gpre_ref[...]) * (1.0 + sc2_ref[...]) + sh2_ref[...]).astype(BF16)
    h2_ref[...] = h2
    qp_ref[...] = jnp.dot(h2, wq_ref[...], preferred_element_type=F32).astype(BF16)


def _postmix(x, ya, yb, gab, gate1, scale2, shift2, g_post1, g_pre2, w_out, w_q, *, tb):
    t = x.shape[0]
    nq = w_q.shape[1]
    if gate1.shape[0] == 1:
        mod_spec = pl.BlockSpec((1, D_MODEL), lambda i: (0, 0))
    else:
        mod_spec = pl.BlockSpec((tb, D_MODEL), lambda i: (i, 0))
    const = lambda shape: pl.BlockSpec(shape, lambda i: (0, 0))
    tok = lambda w: pl.BlockSpec((tb, w), lambda i: (i, 0))
    return pl.pallas_call(
        _postmix_kernel,
        grid=(t // tb,),
        in_specs=[tok(D_MODEL), tok(D_MODEL), tok(D_MODEL), tok(2 * D_MODEL), mod_spec, mod_spec, mod_spec,
                  const((1, D_MODEL)), const((1, D_MODEL)), const((D_MODEL, D_MODEL)), const((D_MODEL, nq))],
        out_specs=[tok(D_MODEL), tok(D_MODEL), tok(nq)],
        out_shape=[jax.ShapeDtypeStruct((t, D_MODEL), F32),
                   jax.ShapeDtypeStruct((t, D_MODEL), BF16),
                   jax.ShapeDtypeStruct((t, nq), BF16)],
        compiler_params=_cparams(("parallel",)),
        name="post_mix",
    )(x, ya, yb, gab, gate1, scale2, shift2, g_post1, g_pre2, w_out, w_q)


NOT_TOP = 255.0


def _top_ranked(s, n):
    vals = []
    work = s
    rank = jnp.full(s.shape, NOT_TOP, F32)
    for i in range(n):
        mx = jnp.max(work, axis=0, keepdims=True)
        hit = work == mx
        rank = jnp.where(hit, float(i), rank)
        work = jnp.where(hit, NEG, work)
        vals.append(mx)
    return vals, rank


def _peer_score_kernel(qp_ref, k1_ref, k2_ref, c1_ref, e1_ref, r2_ref, e2_ref, *, tt):
    ab = N_KEYS // 8
    jrow = lax.broadcasted_iota(jnp.int32, (8, tt), 0)
    for h in range(PEER_HEADS):
        base = h * 2 * N_KEYS
        s1 = _nt(k1_ref[h], qp_ref[:, base:base + N_KEYS])
        s2 = _nt(k2_ref[h], qp_ref[:, base + N_KEYS:base + 2 * N_KEYS])
        t1, rank1 = _top_ranked(s1, PEER_TOPK)
        t2, rank2 = _top_ranked(s2, PEER_TOPK)
        t2a = jnp.concatenate(t2[:8], axis=0)
        t2b = jnp.concatenate(t2[8:], axis=0)
        slabs = [t1[0] + t2a, t1[0] + t2b]
        for i in range(1, PEER_TOPK):
            slabs.append(jnp.where(jrow < PEER_TOPK // (i + 1), t1[i] + t2a, NEG))
        cand = jnp.concatenate(slabs, axis=0)
        tau = _top_ranked(cand, PEER_TOPK)[0][-1]
        top = t1[0] + t2[0]
        sel = cand >= tau
        z = jnp.sum(jnp.where(sel, jnp.exp(cand - top), 0.0), axis=0, keepdims=True)
        ones = jnp.where(sel, 1.0, 0.0)
        counts = [jnp.sum(ones[0:16], axis=0, keepdims=True)]
        for i in range(1, PEER_TOPK):
            counts.append(jnp.sum(ones[8 + 8 * i:16 + 8 * i], axis=0, keepdims=True))
        c1 = jnp.zeros(s1.shape, F32)
        for i in range(PEER_TOPK):
            c1 = jnp.where(rank1 == float(i), counts[i], c1)
        c1_ref[:, h] = c1.reshape(ab, 8, tt)
        e1_ref[:, h] = (jnp.exp(s1 - t1[0]) / z).reshape(ab, 8, tt)
        r2_ref[h] = rank2.astype(BF16)
        e2_ref[h] = jnp.exp(s2 - t2[0]).astype(BF16)


def _peer_scores(qp, k1, k2, *, tt):
    t = qp.shape[0]
    ab = N_KEYS // 8
    a_spec = pl.BlockSpec((ab, PEER_HEADS, 8, tt), lambda i: (0, 0, 0, i))
    b_spec = pl.BlockSpec((PEER_HEADS, N_KEYS, tt), lambda i: (0, 0, i))
    kspec = pl.BlockSpec((PEER_HEADS, N_KEYS, N_KEYS), lambda i: (0, 0, 0))
    a_shape = jax.ShapeDtypeStruct((ab, PEER_HEADS, 8, t), F32)
    b_shape = jax.ShapeDtypeStruct((PEER_HEADS, N_KEYS, t), BF16)
    return pl.pallas_call(
        functools.partial(_peer_score_kernel, tt=tt),
        grid=(t // tt,),
        in_specs=[pl.BlockSpec((tt, qp.shape[1]), lambda i: (i, 0)), kspec, kspec],
        out_specs=[a_spec, a_spec, b_spec, b_spec],
        out_shape=[a_shape, a_shape, b_shape, b_shape],
        compiler_params=_cparams(("parallel",)),
        name="peer_scores",
    )(qp, k1, k2)


def _gelu(x):
    return 0.5 * x * (1.0 + lax.erf(x * (2.0 ** -0.5)))


def _peer_dense_kernel(h2_ref, u_ref, vt_ref, c1_ref, e1_ref, r2_ref, e2_ref, x1_ref, g2_ref, gpost_ref,
                       y_ref, acc_sc, a_sc, *, a_per):
    j = pl.program_id(1)

    @pl.when(j == 0)
    def _():
        acc_sc[...] = jnp.zeros(acc_sc.shape, F32)

    zt = _nt(u_ref[...], h2_ref[...])
    zero = jnp.zeros((), BF16)
    for al in range(a_per):
        w = None
        for h in range(PEER_HEADS):
            c1a = c1_ref[al // 8, h, al % 8:al % 8 + 1, :].astype(BF16)
            e1a = e1_ref[al // 8, h, al % 8:al % 8 + 1, :].astype(BF16)
            term = jnp.where(r2_ref[h] < c1a, e2_ref[h] * e1a, zero)
            w = term if w is None else w + term
        rows = slice(al * N_KEYS, (al + 1) * N_KEYS)
        a_sc[rows, :] = w * _gelu(zt[rows, :]).astype(BF16)
    acc_sc[...] += jnp.dot(vt_ref[...], a_sc[...], preferred_element_type=F32)

    @pl.when(j == pl.num_programs(1) - 1)
    def _():
        f = acc_sc[...].T
        y_ref[...] = x1_ref[...] + g2_ref[...] * _rms(f, gpost_ref[...])


def _peer_dense(h2, u_b, vt_b, c1, e1, r2, e2, x1, gate2, g_post2, *, tt, a_per):
    t = h2.shape[0]
    n_exp = u_b.shape[0]
    eb = a_per * N_KEYS
    if gate2.shape[0] == 1:
        mod_spec = pl.BlockSpec((1, D_MODEL), lambda i, j: (0, 0))
    else:
        mod_spec = pl.BlockSpec((tt, D_MODEL), lambda i, j: (i, 0))
    a_spec = pl.BlockSpec((a_per // 8, PEER_HEADS, 8, tt), lambda i, j: (j, 0, 0, i))
    b_spec = pl.BlockSpec((PEER_HEADS, N_KEYS, tt), lambda i, j: (0, 0, i))
    return pl.pallas_call(
        functools.partial(_peer_dense_kernel, a_per=a_per),
        grid=(t // tt, n_exp // eb),
        in_specs=[pl.BlockSpec((tt, D_MODEL), lambda i, j: (i, 0)),
                  pl.BlockSpec((eb, D_MODEL), lambda i, j: (j, 0)),
                  pl.BlockSpec((D_MODEL, eb), lambda i, j: (0, j)),
                  a_spec, a_spec, b_spec, b_spec,
                  pl.BlockSpec((tt, D_MODEL), lambda i, j: (i, 0)),
                  mod_spec,
                  pl.BlockSpec((1, D_MODEL), lambda i, j: (0, 0))],
        out_specs=pl.BlockSpec((tt, D_MODEL), lambda i, j: (i, 0)),
        out_shape=jax.ShapeDtypeStruct((t, D_MODEL), F32),
        scratch_shapes=[pltpu.VMEM((D_MODEL, tt), F32), pltpu.VMEM((eb, tt), BF16)],
        compiler_params=_cparams(("parallel", "arbitrary")),
        name="peer_dense",
    )(h2, u_b, vt_b, c1, e1, r2, e2, x1, gate2, g_post2)


def _block(t, pref):
    return pref if t % pref == 0 else t


def _layer(x, mods, weights, *, pos0, period, run_mlstm, attend, head_major):
    shift1, scale1, gate1, shift2, scale2, gate2 = mods
    w = weights
    t = x.shape[0]
    tb = _block(t, 256)
    qkv, mo, gab, gates, q1, q2, kf, kb, vf, vb = _inproj(
        x, scale1, shift1, w["g_pre1"], w["w_main"], w["w_if"], w["b_if128"], w["inv128"],
        pos0=pos0, period=period, tb=tb, head_major=head_major)

    ya, c_new, n_new, m_new = run_mlstm(qkv, gates, mo)
    yb = attend(q1, q2, kb, vb)

    x1, h2, qp = _postmix(x, ya, yb, gab, gate1, scale2, shift2, w["g_post1"], w["g_pre2"], w["w_out"], w["w_q"], tb=tb)
    tt = _block(t, 128)
    c1, e1, r2, e2 = _peer_scores(qp, w["k1"], w["k2"], tt=tt)
    ttd = _block(t, 512)
    y = _peer_dense(h2, w["u"], w["vt"], c1, e1, r2, e2, x1, gate2, w["g_post2"], tt=ttd, a_per=8)
    return y, kf, vf, c_new, n_new, m_new


def _hybrid_step(x_prompt, x_sample, cache_k, cache_v, state_C, state_n, state_m, page_table, c_prompt, c_sample,
                 w_ada, b_ada, g_pre1, g_post1, g_pre2, g_post2, w_in, b_if, mlstm_norm,
                 lambda_q1, lambda_k1, lambda_q2, lambda_k2, diff_norm, w_out,
                 peer_wq, peer_k1, peer_k2, peer_u, peer_v, *, m_chunk=128, tq=512, ppb=8):
    l = 0
    bp, sp, _ = x_prompt.shape
    bs, ts, _ = x_sample.shape
    assert bp == 1
    row = lambda a: a.reshape(1, -1)

    wi = w_in[l]
    n_if = 2 * M_HEADS
    w_main = jnp.concatenate([wi[:, :4 * D_MODEL], wi[:, 4 * D_MODEL + n_if:]], axis=1).astype(BF16)
    w_if = jnp.pad(wi[:, 4 * D_MODEL:4 * D_MODEL + n_if], ((0, 0), (0, LANES - n_if))).astype(BF16)
    b_if128 = jnp.pad(b_if[l], (0, LANES - n_if)).reshape(1, LANES)
    inv = ROPE_THETA ** (-jnp.arange(0, A_HEAD_DIM, 2, dtype=F32) / A_HEAD_DIM)
    inv128 = jnp.tile(inv, LANES // inv.shape[0]).reshape(1, LANES)
    weights = dict(
        g_pre1=row(g_pre1[l]), g_post1=row(g_post1[l]), g_pre2=row(g_pre2[l]), g_post2=row(g_post2[l]),
        w_main=w_main, w_if=w_if, b_if128=b_if128, inv128=inv128,
        w_out=w_out[l].astype(BF16), w_q=peer_wq[l].astype(BF16),
        k1=peer_k1[l].astype(BF16), k2=peer_k2[l].astype(BF16),
        u=peer_u[l].astype(BF16), vt=peer_v[l].T.astype(BF16))
    lams = [row(a[l]) for a in (lambda_q1, lambda_k1, lambda_q2, lambda_k2)]
    dn = row(diff_norm[l])
    nw = row(mlstm_norm[l])

    n_c = bp + bs
    c_all = jnp.pad(jnp.concatenate([c_prompt, c_sample], axis=0), ((0, (-n_c) % 8), (0, 0)))
    mod = _ada(c_all, w_ada[l], b_ada[l])
    mods_p = [mod[0:1, i * D_MODEL:(i + 1) * D_MODEL] for i in range(6)]
    mods_s = [jnp.repeat(mod[bp:n_c, i * D_MODEL:(i + 1) * D_MODEL], ts, axis=0) for i in range(6)]

    def run_mlstm_p(qkv, gates, mo):
        c0 = jnp.zeros((bp, M_HEADS, M_HEAD_DIM, M_HEAD_DIM), F32)
        n0 = jnp.zeros((bp, M_HEADS, M_HEAD_DIM), F32)
        m0 = jnp.zeros((bp, M_HEADS, LANES), F32)
        return _mlstm(qkv, gates, mo, nw, c0, n0, m0, batch=bp, L=math.gcd(sp, m_chunk))

    def attend_p(q1t, q2t, kh, vt):
        return _attn_prompt(q1t, q2t, kh, vt, lams, dn, tq=_block(sp, tq))

    yp, kp, vp, cp, n_p, mp = _layer(
        x_prompt.reshape(sp, D_MODEL), mods_p, weights, pos0=0, period=sp,
        run_mlstm=run_mlstm_p, attend=attend_p, head_major=True)

    lpad = 16
    past_len = page_table.shape[1] * cache_k.shape[2]

    def pad_tokens(a, fill_row=None):
        a3 = a.reshape(bs, ts, a.shape[-1])
        if fill_row is None:
            out = jnp.pad(a3, ((0, 0), (0, lpad - ts), (0, 0)))
        else:
            fill = jnp.broadcast_to(fill_row.astype(a.dtype), (bs, lpad - ts, a.shape[-1]))
            out = jnp.concatenate([a3, fill], axis=1)
        return out.reshape(bs * lpad, a.shape[-1])

    def run_mlstm_s(qkv, gates, mo):
        lane = jnp.arange(LANES)
        gate_fill = jnp.where(lane < M_HEADS, NEG, 0.0).astype(F32)
        m0 = jnp.broadcast_to(state_m[l][:, :, None], (bs, M_HEADS, LANES))
        ya, c_new, n_new, m_new = _mlstm(pad_tokens(qkv), pad_tokens(gates, gate_fill), pad_tokens(mo), nw,
                                         state_C[l], state_n[l], m0, batch=bs, L=lpad)
        ya = ya.reshape(bs, lpad, D_MODEL)[:, :ts].reshape(bs * ts, D_MODEL)
        return ya, c_new, n_new, m_new

    def attend_s(q1, q2, kb, vb):
        heads = lambda a: a.reshape(bs, ts, A_HEADS, A_VDIM).transpose(0, 2, 1, 3)
        zeros = lambda n: jnp.zeros((bs, A_HEADS, n, A_VDIM), BF16)
        q_rows = jnp.concatenate([heads(q1), heads(q2), zeros(lpad - 2 * ts)], axis=2)
        k_new = jnp.concatenate([heads(kb), zeros(lpad - ts)], axis=2)
        v_new = jnp.concatenate([heads(vb), zeros(lpad - ts)], axis=2)
        depth, n_pool, page = cache_k.shape[0], cache_k.shape[1], cache_k.shape[2]
        ck = cache_k.reshape(depth * n_pool, page * A_HEADS, A_VDIM)
        cv = cache_v.reshape(depth * n_pool, page * A_HEADS, A_VDIM)
        o = _attn_sample(q_rows, k_new, v_new, ck, cv, page_table, lams, dn,
                         ppb=math.gcd(page_table.shape[1], ppb), n_new=ts, page_base=l * n_pool)
        return o.reshape(bs * ts, D_MODEL)

    ys, k_s, v_s, cs, n_s, ms = _layer(
        x_sample.reshape(bs * ts, D_MODEL), mods_s, weights, pos0=past_len, period=ts,
        run_mlstm=run_mlstm_s, attend=attend_s, head_major=False)

    return (yp.reshape(bp, sp, D_MODEL), ys.reshape(bs, ts, D_MODEL),
            kp.reshape(1, bp, sp, A_HEADS, A_VDIM), vp.reshape(1, bp, sp, A_HEADS, A_VDIM),
            cp[None], n_p[None], mp[None, :, :, 0],
            k_s.reshape(1, bs, ts, A_HEADS, A_VDIM), v_s.reshape(1, bs, ts, A_HEADS, A_VDIM),
            cs[None], n_s[None], ms[None, :, :, 0])


def kernel(x_prompt, x_sample, cache_k, cache_v, state_C, state_n, state_m, page_table, c_prompt, c_sample, w_ada, b_ada, g_pre1, g_post1, g_pre2, g_post2, w_in, b_if, mlstm_norm, lambda_q1, lambda_k1, lambda_q2, lambda_k2, diff_norm, w_out, peer_wq, peer_k1, peer_k2, peer_u, peer_v):
    return _hybrid_step(x_prompt, x_sample, cache_k, cache_v, state_C, state_n, state_m, page_table, c_prompt, c_sample,
                        w_ada, b_ada, g_pre1, g_post1, g_pre2, g_post2, w_in, b_if, mlstm_norm,
                        lambda_q1, lambda_k1, lambda_q2, lambda_k2, diff_norm, w_out,
                        peer_wq, peer_k1, peer_k2, peer_u, peer_v)
```

```python
import functools
import math

import jax
import jax.numpy as jnp
from jax import lax
from jax.experimental import pallas as pl
from jax.experimental.pallas import tpu as pltpu

F32 = jnp.float32
BF16 = jnp.bfloat16

D_MODEL = 1024
M_HEADS = 4
M_HEAD_DIM = 256
A_HEADS = 8
A_HEAD_DIM = 64
A_VDIM = 128
ROPE_THETA = 10000.0
N_KEYS = 128
PEER_HEADS = 8
PEER_TOPK = 16
NORM_EPS = 1e-6
LAMBDA_INIT = 0.8 - 0.6 * math.exp(-0.3 * 0)
N_MAIN_COLS = 9 * D_MODEL

VT_ROWS = A_VDIM + 16
LANES = 128
NEG = -1e30
VMEM_LIMIT = 56 * 1024 * 1024


def _cparams(sem, flags=None):
    return pltpu.CompilerParams(dimension_semantics=sem, vmem_limit_bytes=VMEM_LIMIT, flags=flags)


def _nt(a, b):
    return lax.dot_general(a, b, (((1,), (1,)), ((), ())), preferred_element_type=F32)


def _tn(a, b):
    return lax.dot_general(a, b, (((0,), (0,)), ((), ())), preferred_element_type=F32)


def _rms(x, g):
    return x * lax.rsqrt(jnp.mean(x * x, axis=-1, keepdims=True) + NORM_EPS) * g


def _split3(x):
    x1 = x.astype(BF16)
    r1 = x - x1.astype(F32)
    x2 = r1.astype(BF16)
    x3 = (r1 - x2.astype(F32)).astype(BF16)
    return x1, x2, x3


def _ada_kernel(c_ref, w_ref, b_ref, o_ref):
    c = c_ref[...]
    s = c * jax.nn.sigmoid(c)
    o_ref[...] = jnp.dot(s.astype(BF16), w_ref[...].astype(BF16), preferred_element_type=F32) + b_ref[...]


def _ada(c_all, w_ada, b_ada):
    r = c_all.shape[0]
    n = w_ada.shape[1]
    bn = D_MODEL
    return pl.pallas_call(
        _ada_kernel,
        grid=(n // bn,),
        in_specs=[pl.BlockSpec((r, D_MODEL), lambda j: (0, 0)),
                  pl.BlockSpec((D_MODEL, bn), lambda j: (0, j)),
                  pl.BlockSpec((1, bn), lambda j: (0, j))],
        out_specs=pl.BlockSpec((r, bn), lambda j: (0, j)),
        out_shape=jax.ShapeDtypeStruct((r, n), F32),
        compiler_params=_cparams(("arbitrary",)),
        name="ada_mod",
    )(c_all, w_ada, b_ada.reshape(1, n))


def _rope128(x, cos, sin_signed):
    fwd = pltpu.roll(x, LANES - 32, axis=1)
    bwd = pltpu.roll(x, 32, axis=1)
    lane = lax.broadcasted_iota(jnp.int32, x.shape, 1)
    rot = jnp.where((lane % 64) < 32, fwd, bwd)
    return x * cos + rot * sin_signed


def _inproj_kernel(x_ref, sc_ref, sh_ref, g_ref, wm_ref, wif_ref, bif_ref, inv_ref,
                   qkv_ref, mo_ref, gab_ref, gates_ref, q1_ref, q2_ref, kf_ref, kb_ref, vf_ref, vb_ref,
                   *, tb, pos0, period, head_major):
    x = x_ref[...]
    h = _rms(x, g_ref[...]) * (1.0 + sc_ref[...]) + sh_ref[...]
    hb = h.astype(BF16)

    def proj(i):
        return jnp.dot(hb, wm_ref[:, i * D_MODEL:(i + 1) * D_MODEL], preferred_element_type=F32)

    qkv_ref[:, 0:D_MODEL] = proj(0).astype(BF16)
    qkv_ref[:, D_MODEL:2 * D_MODEL] = (proj(1) * (M_HEAD_DIM ** -0.5)).astype(BF16)
    qkv_ref[:, 2 * D_MODEL:3 * D_MODEL] = proj(2).astype(BF16)
    mo_ref[...] = proj(3)
    gab_ref[:, 0:D_MODEL] = proj(7)
    gab_ref[:, D_MODEL:2 * D_MODEL] = proj(8)

    zif = jnp.dot(hb, wif_ref[...], preferred_element_type=F32) + bif_ref[...]
    lane = lax.broadcasted_iota(jnp.int32, zif.shape, 1)
    logsig = jnp.minimum(zif, 0.0) - jnp.log1p(jnp.exp(-jnp.abs(zif)))
    gates_ref[...] = jnp.where(lane < M_HEADS, zif, logsig)

    row = lax.broadcasted_iota(jnp.int32, (tb, LANES), 0) + pl.program_id(0) * tb
    pos = (pos0 + row % period).astype(F32)
    ang = pos * inv_ref[...]
    cos = jnp.cos(ang)
    sin = jnp.sin(ang)
    lane = lax.broadcasted_iota(jnp.int32, (tb, LANES), 1)
    sin_signed = jnp.where((lane % 64) < 32, -sin, sin)
    first = lane < A_HEAD_DIM

    dq = proj(4)
    dk = proj(5)
    dv = proj(6)
    vf_ref[...] = dv
    if head_major:
        q_scale = A_HEAD_DIM ** -0.5 * math.log2(math.e)
        extra = lax.broadcasted_iota(jnp.int32, (VT_ROWS - A_VDIM, tb), 0)
        ones_rows = jnp.where(extra == 0, 1.0, 0.0).astype(BF16)
    else:
        q_scale = A_HEAD_DIM ** -0.5
        vb_ref[...] = dv.astype(BF16)
    for hd in range(A_HEADS):
        sl = slice(hd * LANES, (hd + 1) * LANES)
        qr = _rope128(dq[:, sl], cos, sin_signed) * q_scale
        q1 = jnp.where(first, qr, 0.0)
        q2 = jnp.where(first, 0.0, qr)
        kr = _rope128(dk[:, sl], cos, sin_signed)
        kf_ref[:, sl] = kr
        if head_major:
            q1_ref[hd] = q1.T.astype(BF16)
            q2_ref[hd] = q2.T.astype(BF16)
            kb_ref[hd] = kr.astype(BF16)
            vb_ref[hd, 0:A_VDIM, :] = dv[:, sl].T.astype(BF16)
            vb_ref[hd, A_VDIM:VT_ROWS, :] = ones_rows
        else:
            q1_ref[:, sl] = q1.astype(BF16)
            q2_ref[:, sl] = q2.astype(BF16)
            kb_ref[:, sl] = kr.astype(BF16)


def _inproj(x, scale, shift, g_pre, w_main, w_if, b_if128, inv128, *, pos0, period, tb, head_major):
    t = x.shape[0]
    mod_rows = scale.shape[0]
    if mod_rows == 1:
        mod_spec = pl.BlockSpec((1, D_MODEL), lambda i: (0, 0))
    else:
        mod_spec = pl.BlockSpec((tb, D_MODEL), lambda i: (i, 0))
    const = lambda shape: pl.BlockSpec(shape, lambda i: (0, 0))
    tok = lambda w: (pl.BlockSpec((tb, w), lambda i: (i, 0)), (t, w))
    if head_major:
        feat_t = lambda rows: (pl.BlockSpec((A_HEADS, rows, tb), lambda i: (0, 0, i)), (A_HEADS, rows, t))
        tok_h = (pl.BlockSpec((A_HEADS, tb, A_VDIM), lambda i: (0, i, 0)), (A_HEADS, t, A_VDIM))
        q_out, k_out, v_out = feat_t(A_VDIM), tok_h, feat_t(VT_ROWS)
    else:
        q_out = k_out = v_out = tok(D_MODEL)
    outs = [
        (tok(3 * D_MODEL), BF16),
        (tok(D_MODEL), F32),
        (tok(2 * D_MODEL), F32),
        (tok(LANES), F32),
        (q_out, BF16),
        (q_out, BF16),
        (tok(D_MODEL), F32),
        (k_out, BF16),
        (tok(D_MODEL), F32),
        (v_out, BF16),
    ]
    return pl.pallas_call(
        functools.partial(_inproj_kernel, tb=tb, pos0=pos0, period=period, head_major=head_major),
        grid=(t // tb,),
        in_specs=[tok(D_MODEL)[0], mod_spec, mod_spec, const((1, D_MODEL)),
                  const((D_MODEL, N_MAIN_COLS)), const((D_MODEL, LANES)), const((1, LANES)), const((1, LANES))],
        out_specs=[spec for (spec, _), _ in outs],
        out_shape=[jax.ShapeDtypeStruct(shape, dt) for (_, shape), dt in outs],
        compiler_params=_cparams(("parallel",)),
        name="in_proj",
    )(x, scale, shift, g_pre, w_main, w_if, b_if128, inv128)


def _mlstm_kernel(qkv_ref, gates_ref, mo_ref, nw_ref, c0_ref, n0_ref, m0_ref,
                  ya_ref, c_ref, n_ref, m_ref, *, L):
    @pl.when(pl.program_id(1) == 0)
    def _():
        c_ref[...] = c0_ref[...]
        n_ref[...] = n0_ref[...]
        m_ref[...] = m0_ref[...]

    gates = gates_ref[...]
    r = lax.broadcasted_iota(jnp.int32, (L, L), 0)
    c = lax.broadcasted_iota(jnp.int32, (L, L), 1)
    causal = c <= r
    tril = jnp.where(causal, 1.0, 0.0).astype(BF16)
    eye8 = jnp.where(lax.broadcasted_iota(jnp.int32, (8, LANES), 0)
                     == lax.broadcasted_iota(jnp.int32, (8, LANES), 1), 1.0, 0.0).astype(BF16)
    g3 = _split3(gates)
    cums = sum(jnp.dot(tril, gi, preferred_element_type=F32) for gi in g3)
    gt8 = sum(_nt(eye8, gi) for gi in g3)
    bt8 = sum(_nt(eye8, ci) for ci in _split3(cums))

    for h in range(M_HEADS):
        sl = slice(h * M_HEAD_DIM, (h + 1) * M_HEAD_DIM)
        q = qkv_ref[:, h * M_HEAD_DIM:(h + 1) * M_HEAD_DIM]
        k = qkv_ref[:, D_MODEL + h * M_HEAD_DIM:D_MODEL + (h + 1) * M_HEAD_DIM]
        v = qkv_ref[:, 2 * D_MODEL + h * M_HEAD_DIM:2 * D_MODEL + (h + 1) * M_HEAD_DIM]
        ig_col = gates[:, h:h + 1]
        ig_row = gt8[h:h + 1, :]
        b_col = cums[:, M_HEADS + h:M_HEADS + h + 1]
        b_row = bt8[M_HEADS + h:M_HEADS + h + 1, :]
        m_prev = m_ref[0, h:h + 1, 0:1]
        c_old = c_ref[0, h]
        n_old = n_ref[0, h:h + 1, :]

        dmat = jnp.where(causal, b_col - b_row + ig_row, NEG)
        m_inter = b_col + m_prev
        m_t = jnp.maximum(m_inter, jnp.max(dmat, axis=-1, keepdims=True))
        w_inter = jnp.exp(m_inter - m_t)
        a = jnp.exp(dmat - m_t) * _nt(q, k)
        num = w_inter * _nt(q, c_old.astype(BF16)) + jnp.dot(a.astype(BF16), v, preferred_element_type=F32)
        qn = jnp.sum(q.astype(F32) * n_old, axis=-1, keepdims=True)
        den = w_inter * qn + jnp.sum(a, axis=-1, keepdims=True)
        hm = num / jnp.maximum(jnp.abs(den), jnp.exp(-m_t))

        b_last = b_col[L - 1:L, :]
        g_row = b_last - b_row + ig_row
        g_col = b_last - b_col + ig_col
        m_new = jnp.maximum(b_last + m_prev, jnp.max(g_row, axis=-1, keepdims=True))
        dec = jnp.exp(b_last + m_prev - m_new)
        w_col = jnp.exp(g_col - m_new)
        vw = (v.astype(F32) * w_col).astype(BF16)
        c_ref[0, h] = dec * c_old + _tn(vw, k)
        n_ref[0, h:h + 1, :] = dec * n_old + jnp.sum(k.astype(F32) * w_col, axis=0, keepdims=True)
        m_ref[0, h:h + 1, :] = jnp.broadcast_to(m_new, (1, LANES))

        ya_ref[:, sl] = _rms(hm, nw_ref[:, sl]) * jax.nn.sigmoid(mo_ref[:, sl])


def _mlstm(qkv, gates, mo, norm_w, c0, n0, m0, *, batch, L):
    t_total = qkv.shape[0]
    nc = t_total // (batch * L)
    tok = lambda w: pl.BlockSpec((L, w), lambda b, c: (b * nc + c, 0))
    st4 = pl.BlockSpec((1, M_HEADS, M_HEAD_DIM, M_HEAD_DIM), lambda b, c: (b, 0, 0, 0))
    st3n = pl.BlockSpec((1, M_HEADS, M_HEAD_DIM), lambda b, c: (b, 0, 0))
    st3m = pl.BlockSpec((1, M_HEADS, LANES), lambda b, c: (b, 0, 0))
    return pl.pallas_call(
        functools.partial(_mlstm_kernel, L=L),
        grid=(batch, nc),
        in_specs=[tok(3 * D_MODEL), tok(LANES), tok(D_MODEL), pl.BlockSpec((1, D_MODEL), lambda b, c: (0, 0)),
                  st4, st3n, st3m],
        out_specs=[tok(D_MODEL), st4, st3n, st3m],
        out_shape=[jax.ShapeDtypeStruct((t_total, D_MODEL), F32),
                   jax.ShapeDtypeStruct(c0.shape, F32),
                   jax.ShapeDtypeStruct(n0.shape, F32),
                   jax.ShapeDtypeStruct(m0.shape, F32)],
        compiler_params=_cparams(("parallel", "arbitrary")),
        name="mlstm",
    )(qkv, gates, mo, norm_w, c0, n0, m0)


def _lambda_full(lq1_ref, lk1_ref, lq2_ref, lk2_ref):
    a = jnp.exp(jnp.sum(lq1_ref[...] * lk1_ref[...], axis=-1, keepdims=True))
    b = jnp.exp(jnp.sum(lq2_ref[...] * lk2_ref[...], axis=-1, keepdims=True))
    return a - b + LAMBDA_INIT


def _attn_prompt_kernel(q1_ref, q2_ref, k_ref, vt_ref, lq1_ref, lk1_ref, lq2_ref, lk2_ref, dn_ref,
                        o_ref, m_sc, a_sc, acc_sc, s_sc, p_sc, *, tq):
    qi = pl.program_id(0)
    ki = pl.program_id(1)
    n_pairs = 2 * A_HEADS
    q_refs = (q1_ref, q2_ref)

    @pl.when(ki == 0)
    def _():
        m_sc[...] = jnp.full(m_sc.shape, NEG, F32)
        acc_sc[...] = jnp.zeros(acc_sc.shape, F32)

    def scores(head, mp):
        s_sc[mp] = jnp.dot(k_ref[head], q_refs[mp][head], preferred_element_type=F32)

    def softmax(pair, mp, masked):
        s = s_sc[mp]
        if masked:
            key = lax.broadcasted_iota(jnp.int32, (tq, tq), 0)
            qry = lax.broadcasted_iota(jnp.int32, (tq, tq), 1)
            s = jnp.where(key <= qry, s, NEG)
        m_prev = m_sc[pair]
        m_new = jnp.maximum(m_prev, jnp.max(s, axis=0, keepdims=True))
        a_sc[pair] = jnp.exp2(m_prev - m_new)
        p_sc[mp] = jnp.exp2(s - m_new).astype(BF16)
        m_sc[pair] = m_new

    def values(pair, head, mp):
        acc_sc[pair] = a_sc[pair] * acc_sc[pair] + jnp.dot(vt_ref[head], p_sc[mp], preferred_element_type=F32)

    def run(masked):
        scores(0, 0)
        scores(0, 1)
        softmax(0, 0, masked)

        def body(head, carry):
            pair = 2 * head
            values(pair, head, 0)
            softmax(pair + 1, 1, masked)
            scores(head + 1, 0)
            values(pair + 1, head, 1)
            softmax(pair + 2, 0, masked)
            scores(head + 1, 1)
            return carry

        lax.fori_loop(0, A_HEADS - 1, body, 0)
        softmax(n_pairs - 1, 1, masked)
        values(n_pairs - 2, A_HEADS - 1, 0)
        values(n_pairs - 1, A_HEADS - 1, 1)

    @pl.when(ki < qi)
    def _():
        run(False)

    @pl.when(ki == qi)
    def _():
        run(True)
        lam = _lambda_full(lq1_ref, lk1_ref, lq2_ref, lk2_ref)
        for h in range(A_HEADS):
            a1 = acc_sc[2 * h]
            a2 = acc_sc[2 * h + 1]
            ot = (a1[0:A_VDIM] / a1[A_VDIM:A_VDIM + 1]
                  - lam * (a2[0:A_VDIM] / a2[A_VDIM:A_VDIM + 1]))
            o_ref[:, h * LANES:(h + 1) * LANES] = _rms(ot.T, dn_ref[...]) * (1.0 - LAMBDA_INIT)


def _attn_prompt(q1t, q2t, kh, vt, lams, diff_norm, *, tq):
    t = kh.shape[1]
    nq = t // tq
    qspec = pl.BlockSpec((A_HEADS, A_VDIM, tq), lambda i, j: (0, 0, i))
    kspec = pl.BlockSpec((A_HEADS, tq, A_VDIM), lambda i, j: (0, jnp.minimum(i, j), 0))
    vspec = pl.BlockSpec((A_HEADS, VT_ROWS, tq), lambda i, j: (0, 0, jnp.minimum(i, j)))
    small = lambda n: pl.BlockSpec((1, n), lambda i, j: (0, 0))
    return pl.pallas_call(
        functools.partial(_attn_prompt_kernel, tq=tq),
        grid=(nq, nq),
        in_specs=[qspec, qspec, kspec, vspec] + [small(A_HEAD_DIM)] * 4 + [small(A_VDIM)],
        out_specs=pl.BlockSpec((tq, D_MODEL), lambda i, j: (i, 0)),
        out_shape=jax.ShapeDtypeStruct((t, D_MODEL), F32),
        scratch_shapes=[pltpu.VMEM((2 * A_HEADS, 1, tq), F32),
                        pltpu.VMEM((2 * A_HEADS, 1, tq), F32),
                        pltpu.VMEM((2 * A_HEADS, VT_ROWS, tq), F32),
                        pltpu.VMEM((2, tq, tq), F32), pltpu.VMEM((2, tq, tq), BF16)],
        compiler_params=_cparams(("parallel", "arbitrary")),
        name="diff_attn_prompt",
    )(q1t, q2t, kh, vt, *lams, diff_norm)


def _attn_sample_kernel(pt_ref, q_ref, kn_ref, vn_ref, lq1_ref, lk1_ref, lq2_ref, lk2_ref, dn_ref, *rest,
                        ppb, n_new, page):
    k_refs = rest[:ppb]
    v_refs = rest[ppb:2 * ppb]
    o_ref, m_sc, l_sc, acc_sc = rest[2 * ppb:]
    j = pl.program_id(1)
    rows = q_ref.shape[2]

    @pl.when(j == 0)
    def _():
        m_sc[...] = jnp.full(m_sc.shape, NEG, F32)
        l_sc[...] = jnp.zeros(l_sc.shape, F32)
        acc_sc[...] = jnp.zeros(acc_sc.shape, F32)

    def update(scores, values):
        s = jnp.concatenate([scores(h) for h in range(A_HEADS)], axis=0)
        m_prev = m_sc[...]
        m_new = jnp.maximum(m_prev, jnp.max(s, axis=-1, keepdims=True))
        alpha = jnp.exp(m_prev - m_new)
        p = jnp.exp(s - m_new)
        l_sc[...] = alpha * l_sc[...] + jnp.sum(p, axis=-1, keepdims=True)
        pb = p.astype(BF16)
        pv = jnp.concatenate([jnp.dot(pb[h * rows:(h + 1) * rows], values(h), preferred_element_type=F32)
                              for h in range(A_HEADS)], axis=0)
        acc_sc[...] = alpha * acc_sc[...] + pv
        m_sc[...] = m_new

    def head_rows(refs, h):
        parts = [r[0, pl.ds(h, page, stride=A_HEADS), :] for r in refs]
        return jnp.concatenate(parts, axis=0).astype(BF16)

    update(lambda h: _nt(q_ref[0, h], head_rows(k_refs, h)), lambda h: head_rows(v_refs, h))

    @pl.when(j == pl.num_programs(1) - 1)
    def _():
        lam = _lambda_full(lq1_ref, lk1_ref, lq2_ref, lk2_ref)
        r = lax.broadcasted_iota(jnp.int32, (rows, rows), 0)
        c = lax.broadcasted_iota(jnp.int32, (rows, rows), 1)
        keep = (c <= r % n_new) & (c < n_new)
        update(lambda h: jnp.where(keep, _nt(q_ref[0, h], kn_ref[0, h]), NEG), lambda h: vn_ref[0, h])
        o = acc_sc[...] / l_sc[...]
        for h in range(A_HEADS):
            od = o[h * rows:h * rows + n_new, :] - lam * o[h * rows + n_new:h * rows + 2 * n_new, :]
            o_ref[0, :, h * LANES:(h + 1) * LANES] = _rms(od, dn_ref[...]) * (1.0 - LAMBDA_INIT)


def _attn_sample(q_rows, k_new, v_new, cache_k, cache_v, page_table, lams, diff_norm, *, ppb, n_new, page_base):
    bsz, n_pages = page_table.shape
    page = cache_k.shape[1] // A_HEADS
    rows = q_rows.shape[2]
    seq = lambda: pl.BlockSpec((1, A_HEADS, rows, A_VDIM), lambda b, j, pt: (b, 0, 0, 0))
    small = lambda n: pl.BlockSpec((1, n), lambda b, j, pt: (0, 0))

    def page_spec(i):
        return pl.BlockSpec((1, page * A_HEADS, A_VDIM), lambda b, j, pt: (page_base + pt[b, j * ppb + i], 0, 0))

    grid_spec = pltpu.PrefetchScalarGridSpec(
        num_scalar_prefetch=1,
        grid=(bsz, n_pages // ppb),
        in_specs=[seq(), seq(), seq()] + [small(A_HEAD_DIM)] * 4 + [small(A_VDIM)]
                 + [page_spec(i) for i in range(ppb)] + [page_spec(i) for i in range(ppb)],
        out_specs=pl.BlockSpec((1, n_new, D_MODEL), lambda b, j, pt: (b, 0, 0)),
        scratch_shapes=[pltpu.VMEM((A_HEADS * rows, 1), F32),
                        pltpu.VMEM((A_HEADS * rows, 1), F32),
                        pltpu.VMEM((A_HEADS * rows, A_VDIM), F32)],
    )
    return pl.pallas_call(
        functools.partial(_attn_sample_kernel, ppb=ppb, n_new=n_new, page=page),
        grid_spec=grid_spec,
        out_shape=jax.ShapeDtypeStruct((bsz, n_new, D_MODEL), F32),
        compiler_params=_cparams(("parallel", "arbitrary")),
        name="diff_attn_sample",
    )(page_table, q_rows, k_new, v_new, *lams, diff_norm, *([cache_k] * ppb), *([cache_v] * ppb))


def _postmix_kernel(x_ref, ya_ref, yb_ref, gab_ref, g1_ref, sc2_ref, sh2_ref, gpost_ref, gpre_ref, wo_ref, wq_ref,
                    x1_ref, h2_ref, qp_ref):
    merged = (jax.nn.sigmoid(gab_ref[:, 0:D_MODEL]) * ya_ref[...]
              + jax.nn.sigmoid(gab_ref[:, D_MODEL:2 * D_MODEL]) * yb_ref[...])
    o = jnp.dot(merged.astype(BF16), wo_ref[...], preferred_element_type=F32)
    x1 = x_ref[...] + g1_ref[...] * _rms(o, gpost_ref[...])
    x1_ref[...] = x1
    h2 = (_rms(x1, gpre_ref[...]) * (1.0 + sc2_ref[...]) + sh2_ref[...]).astype(BF16)
    h2_ref[...] = h2
    qp_ref[...] = jnp.dot(h2, wq_ref[...], preferred_element_type=F32).astype(BF16)


def _postmix(x, ya, yb, gab, gate1, scale2, shift2, g_post1, g_pre2, w_out, w_q, *, tb):
    t = x.shape[0]
    nq = w_q.shape[1]
    if gate1.shape[0] == 1:
        mod_spec = pl.BlockSpec((1, D_MODEL), lambda i: (0, 0))
    else:
        mod_spec = pl.BlockSpec((tb, D_MODEL), lambda i: (i, 0))
    const = lambda shape: pl.BlockSpec(shape, lambda i: (0, 0))
    tok = lambda w: pl.BlockSpec((tb, w), lambda i: (i, 0))
    return pl.pallas_call(
        _postmix_kernel,
        grid=(t // tb,),
        in_specs=[tok(D_MODEL), tok(D_MODEL), tok(D_MODEL), tok(2 * D_MODEL), mod_spec, mod_spec, mod_spec,
                  const((1, D_MODEL)), const((1, D_MODEL)), const((D_MODEL, D_MODEL)), const((D_MODEL, nq))],
        out_specs=[tok(D_MODEL), tok(D_MODEL), tok(nq)],
        out_shape=[jax.ShapeDtypeStruct((t, D_MODEL), F32),
                   jax.ShapeDtypeStruct((t, D_MODEL), BF16),
                   jax.ShapeDtypeStruct((t, nq), BF16)],
        compiler_params=_cparams(("parallel",)),
        name="post_mix",
    )(x, ya, yb, gab, gate1, scale2, shift2, g_post1, g_pre2, w_out, w_q)


NOT_TOP = 255.0


def _top_ranked(s, n):
    vals = []
    work = s
    rank = jnp.full(s.shape, NOT_TOP, F32)
    for i in range(n):
        mx = jnp.max(work, axis=0, keepdims=True)
        hit = work == mx
        rank = jnp.where(hit, float(i), rank)
        work = jnp.where(hit, NEG, work)
        vals.append(mx)
    return vals, rank


def _peer_score_kernel(qp_ref, k1_ref, k2_ref, c1_ref, e1_ref, r2_ref, e2_ref, *, tt):
    ab = N_KEYS // 8
    jrow = lax.broadcasted_iota(jnp.int32, (8, tt), 0)
    for h in range(PEER_HEADS):
        base = h * 2 * N_KEYS
        s1 = _nt(k1_ref[h], qp_ref[:, base:base + N_KEYS])
        s2 = _nt(k2_ref[h], qp_ref[:, base + N_KEYS:base + 2 * N_KEYS])
        t1, rank1 = _top_ranked(s1, PEER_TOPK)
        t2, rank2 = _top_ranked(s2, PEER_TOPK)
        t2a = jnp.concatenate(t2[:8], axis=0)
        t2b = jnp.concatenate(t2[8:], axis=0)
        slabs = [t1[0] + t2a, t1[0] + t2b]
        for i in range(1, PEER_TOPK):
            slabs.append(jnp.where(jrow < PEER_TOPK // (i + 1), t1[i] + t2a, NEG))
        cand = jnp.concatenate(slabs, axis=0)
        tau = _top_ranked(cand, PEER_TOPK)[0][-1]
        top = t1[0] + t2[0]
        sel = cand >= tau
        z = jnp.sum(jnp.where(sel, jnp.exp(cand - top), 0.0), axis=0, keepdims=True)
        ones = jnp.where(sel, 1.0, 0.0)
        counts = [jnp.sum(ones[0:16], axis=0, keepdims=True)]
        for i in range(1, PEER_TOPK):
            counts.append(jnp.sum(ones[8 + 8 * i:16 + 8 * i], axis=0, keepdims=True))
        c1 = jnp.zeros(s1.shape, F32)
        for i in range(PEER_TOPK):
            c1 = jnp.where(rank1 == float(i), counts[i], c1)
        c1_ref[:, h] = c1.reshape(ab, 8, tt)
        e1_ref[:, h] = (jnp.exp(s1 - t1[0]) / z).reshape(ab, 8, tt)
        r2_ref[h] = rank2.astype(BF16)
        e2_ref[h] = jnp.exp(s2 - t2[0]).astype(BF16)


def _peer_scores(qp, k1, k2, *, tt):
    t = qp.shape[0]
    ab = N_KEYS // 8
    a_spec = pl.BlockSpec((ab, PEER_HEADS, 8, tt), lambda i: (0, 0, 0, i))
    b_spec = pl.BlockSpec((PEER_HEADS, N_KEYS, tt), lambda i: (0, 0, i))
    kspec = pl.BlockSpec((PEER_HEADS, N_KEYS, N_KEYS), lambda i: (0, 0, 0))
    a_shape = jax.ShapeDtypeStruct((ab, PEER_HEADS, 8, t), F32)
    b_shape = jax.ShapeDtypeStruct((PEER_HEADS, N_KEYS, t), BF16)
    return pl.pallas_call(
        functools.partial(_peer_score_kernel, tt=tt),
        grid=(t // tt,),
        in_specs=[pl.BlockSpec((tt, qp.shape[1]), lambda i: (i, 0)), kspec, kspec],
        out_specs=[a_spec, a_spec, b_spec, b_spec],
        out_shape=[a_shape, a_shape, b_shape, b_shape],
        compiler_params=_cparams(("parallel",)),
        name="peer_scores",
    )(qp, k1, k2)


def _gelu(x):
    return 0.5 * x * (1.0 + lax.erf(x * (2.0 ** -0.5)))


def _peer_dense_kernel(h2_ref, u_ref, vt_ref, c1_ref, e1_ref, r2_ref, e2_ref, x1_ref, g2_ref, gpost_ref,
                       y_ref, acc_sc, a_sc, *, a_per):
    j = pl.program_id(1)

    @pl.when(j == 0)
    def _():
        acc_sc[...] = jnp.zeros(acc_sc.shape, F32)

    zt = _nt(u_ref[...], h2_ref[...])
    zero = jnp.zeros((), BF16)
    tt = h2_ref.shape[0]

    def key_rows(ref, al, h):
        row = ref[al // 8, h, al % 8:al % 8 + 1, :]
        tile = jnp.broadcast_to(row, (16, tt)).astype(BF16)
        return pltpu.repeat(tile, N_KEYS // 16, axis=0)

    for al in range(a_per):
        w = None
        for h in range(PEER_HEADS):
            term = jnp.where(r2_ref[h] < key_rows(c1_ref, al, h), e2_ref[h] * key_rows(e1_ref, al, h), zero)
            w = term if w is None else w + term
        rows = slice(al * N_KEYS, (al + 1) * N_KEYS)
        a_sc[rows, :] = w * _gelu(zt[rows, :]).astype(BF16)
    acc_sc[...] += jnp.dot(vt_ref[...], a_sc[...], preferred_element_type=F32)

    @pl.when(j == pl.num_programs(1) - 1)
    def _():
        f = acc_sc[...].T
        y_ref[...] = x1_ref[...] + g2_ref[...] * _rms(f, gpost_ref[...])


def _peer_dense(h2, u_b, vt_b, c1, e1, r2, e2, x1, gate2, g_post2, *, tt, a_per):
    t = h2.shape[0]
    n_exp = u_b.shape[0]
    eb = a_per * N_KEYS
    if gate2.shape[0] == 1:
        mod_spec = pl.BlockSpec((1, D_MODEL), lambda i, j: (0, 0))
    else:
        mod_spec = pl.BlockSpec((tt, D_MODEL), lambda i, j: (i, 0))
    a_spec = pl.BlockSpec((a_per // 8, PEER_HEADS, 8, tt), lambda i, j: (j, 0, 0, i))
    b_spec = pl.BlockSpec((PEER_HEADS, N_KEYS, tt), lambda i, j: (0, 0, i))
    return pl.pallas_call(
        functools.partial(_peer_dense_kernel, a_per=a_per),
        grid=(t // tt, n_exp // eb),
        in_specs=[pl.BlockSpec((tt, D_MODEL), lambda i, j: (i, 0)),
                  pl.BlockSpec((eb, D_MODEL), lambda i, j: (j, 0)),
                  pl.BlockSpec((D_MODEL, eb), lambda i, j: (0, j)),
                  a_spec, a_spec, b_spec, b_spec,
                  pl.BlockSpec((tt, D_MODEL), lambda i, j: (i, 0)),
                  mod_spec,
                  pl.BlockSpec((1, D_MODEL), lambda i, j: (0, 0))],
        out_specs=pl.BlockSpec((tt, D_MODEL), lambda i, j: (i, 0)),
        out_shape=jax.ShapeDtypeStruct((t, D_MODEL), F32),
        scratch_shapes=[pltpu.VMEM((D_MODEL, tt), F32), pltpu.VMEM((eb, tt), BF16)],
        compiler_params=_cparams(("parallel", "arbitrary")),
        name="peer_dense",
    )(h2, u_b, vt_b, c1, e1, r2, e2, x1, gate2, g_post2)


def _block(t, pref):
    return pref if t % pref == 0 else t


def _layer(x, mods, weights, *, pos0, period, run_mlstm, attend, head_major):
    shift1, scale1, gate1, shift2, scale2, gate2 = mods
    w = weights
    t = x.shape[0]
    tb = _block(t, 256)
    qkv, mo, gab, gates, q1, q2, kf, kb, vf, vb = _inproj(
        x, scale1, shift1, w["g_pre1"], w["w_main"], w["w_if"], w["b_if128"], w["inv128"],
        pos0=pos0, period=period, tb=tb, head_major=head_major)

    ya, c_new, n_new, m_new = run_mlstm(qkv, gates, mo)
    yb = attend(q1, q2, kb, vb)

    x1, h2, qp = _postmix(x, ya, yb, gab, gate1, scale2, shift2, w["g_post1"], w["g_pre2"], w["w_out"], w["w_q"], tb=tb)
    tt = _block(t, 128)
    c1, e1, r2, e2 = _peer_scores(qp, w["k1"], w["k2"], tt=tt)
    ttd = _block(t, 512)
    y = _peer_dense(h2, w["u"], w["vt"], c1, e1, r2, e2, x1, gate2, w["g_post2"], tt=ttd, a_per=8)
    return y, kf, vf, c_new, n_new, m_new


def _hybrid_step(x_prompt, x_sample, cache_k, cache_v, state_C, state_n, state_m, page_table, c_prompt, c_sample,
                 w_ada, b_ada, g_pre1, g_post1, g_pre2, g_post2, w_in, b_if, mlstm_norm,
                 lambda_q1, lambda_k1, lambda_q2, lambda_k2, diff_norm, w_out,
                 peer_wq, peer_k1, peer_k2, peer_u, peer_v, *, m_chunk=128, tq=512, ppb=8):
    l = 0
    bp, sp, _ = x_prompt.shape
    bs, ts, _ = x_sample.shape
    assert bp == 1
    row = lambda a: a.reshape(1, -1)

    wi = w_in[l]
    n_if = 2 * M_HEADS
    w_main = jnp.concatenate([wi[:, :4 * D_MODEL], wi[:, 4 * D_MODEL + n_if:]], axis=1).astype(BF16)
    w_if = jnp.pad(wi[:, 4 * D_MODEL:4 * D_MODEL + n_if], ((0, 0), (0, LANES - n_if))).astype(BF16)
    b_if128 = jnp.pad(b_if[l], (0, LANES - n_if)).reshape(1, LANES)
    inv = ROPE_THETA ** (-jnp.arange(0, A_HEAD_DIM, 2, dtype=F32) / A_HEAD_DIM)
    inv128 = jnp.tile(inv, LANES // inv.shape[0]).reshape(1, LANES)
    weights = dict(
        g_pre1=row(g_pre1[l]), g_post1=row(g_post1[l]), g_pre2=row(g_pre2[l]), g_post2=row(g_post2[l]),
        w_main=w_main, w_if=w_if, b_if128=b_if128, inv128=inv128,
        w_out=w_out[l].astype(BF16), w_q=peer_wq[l].astype(BF16),
        k1=peer_k1[l].astype(BF16), k2=peer_k2[l].astype(BF16),
        u=peer_u[l].astype(BF16), vt=peer_v[l].T.astype(BF16))
    lams = [row(a[l]) for a in (lambda_q1, lambda_k1, lambda_q2, lambda_k2)]
    dn = row(diff_norm[l])
    nw = row(mlstm_norm[l])

    n_c = bp + bs
    c_all = jnp.pad(jnp.concatenate([c_prompt, c_sample], axis=0), ((0, (-n_c) % 8), (0, 0)))
    mod = _ada(c_all, w_ada[l], b_ada[l])
    mods_p = [mod[0:1, i * D_MODEL:(i + 1) * D_MODEL] for i in range(6)]
    mods_s = [jnp.repeat(mod[bp:n_c, i * D_MODEL:(i + 1) * D_MODEL], ts, axis=0) for i in range(6)]

    def run_mlstm_p(qkv, gates, mo):
        c0 = jnp.zeros((bp, M_HEADS, M_HEAD_DIM, M_HEAD_DIM), F32)
        n0 = jnp.zeros((bp, M_HEADS, M_HEAD_DIM), F32)
        m0 = jnp.zeros((bp, M_HEADS, LANES), F32)
        return _mlstm(qkv, gates, mo, nw, c0, n0, m0, batch=bp, L=math.gcd(sp, m_chunk))

    def attend_p(q1t, q2t, kh, vt):
        return _attn_prompt(q1t, q2t, kh, vt, lams, dn, tq=_block(sp, tq))

    yp, kp, vp, cp, n_p, mp = _layer(
        x_prompt.reshape(sp, D_MODEL), mods_p, weights, pos0=0, period=sp,
        run_mlstm=run_mlstm_p, attend=attend_p, head_major=True)

    lpad = 16
    past_len = page_table.shape[1] * cache_k.shape[2]

    def pad_tokens(a, fill_row=None):
        a3 = a.reshape(bs, ts, a.shape[-1])
        if fill_row is None:
            out = jnp.pad(a3, ((0, 0), (0, lpad - ts), (0, 0)))
        else:
            fill = jnp.broadcast_to(fill_row.astype(a.dtype), (bs, lpad - ts, a.shape[-1]))
            out = jnp.concatenate([a3, fill], axis=1)
        return out.reshape(bs * lpad, a.shape[-1])

    def run_mlstm_s(qkv, gates, mo):
        lane = jnp.arange(LANES)
        gate_fill = jnp.where(lane < M_HEADS, NEG, 0.0).astype(F32)
        m0 = jnp.broadcast_to(state_m[l][:, :, None], (bs, M_HEADS, LANES))
        ya, c_new, n_new, m_new = _mlstm(pad_tokens(qkv), pad_tokens(gates, gate_fill), pad_tokens(mo), nw,
                                         state_C[l], state_n[l], m0, batch=bs, L=lpad)
        ya = ya.reshape(bs, lpad, D_MODEL)[:, :ts].reshape(bs * ts, D_MODEL)
        return ya, c_new, n_new, m_new

    def attend_s(q1, q2, kb, vb):
        heads = lambda a: a.reshape(bs, ts, A_HEADS, A_VDIM).transpose(0, 2, 1, 3)
        zeros = lambda n: jnp.zeros((bs, A_HEADS, n, A_VDIM), BF16)
        q_rows = jnp.concatenate([heads(q1), heads(q2), zeros(lpad - 2 * ts)], axis=2)
        k_new = jnp.concatenate([heads(kb), zeros(lpad - ts)], axis=2)
        v_new = jnp.concatenate([heads(vb), zeros(lpad - ts)], axis=2)
        depth, n_pool, page = cache_k.shape[0], cache_k.shape[1], cache_k.shape[2]
        ck = cache_k.reshape(depth * n_pool, page * A_HEADS, A_VDIM)
        cv = cache_v.reshape(depth * n_pool, page * A_HEADS, A_VDIM)
        o = _attn_sample(q_rows, k_new, v_new, ck, cv, page_table, lams, dn,
                         ppb=math.gcd(page_table.shape[1], ppb), n_new=ts, page_base=l * n_pool)
        return o.reshape(bs * ts, D_MODEL)

    ys, k_s, v_s, cs, n_s, ms = _layer(
        x_sample.reshape(bs * ts, D_MODEL), mods_s, weights, pos0=past_len, period=ts,
        run_mlstm=run_mlstm_s, attend=attend_s, head_major=False)

    return (yp.reshape(bp, sp, D_MODEL), ys.reshape(bs, ts, D_MODEL),
            kp.reshape(1, bp, sp, A_HEADS, A_VDIM), vp.reshape(1, bp, sp, A_HEADS, A_VDIM),
            cp[None], n_p[None], mp[None, :, :, 0],
            k_s.reshape(1, bs, ts, A_HEADS, A_VDIM), v_s.reshape(1, bs, ts, A_HEADS, A_VDIM),
            cs[None], n_s[None], ms[None, :, :, 0])


def kernel(x_prompt, x_sample, cache_k, cache_v, state_C, state_n, state_m, page_table, c_prompt, c_sample, w_ada, b_ada, g_pre1, g_post1, g_pre2, g_post2, w_in, b_if, mlstm_norm, lambda_q1, lambda_k1, lambda_q2, lambda_k2, diff_norm, w_out, peer_wq, peer_k1, peer_k2, peer_u, peer_v):
    return _hybrid_step(x_prompt, x_sample, cache_k, cache_v, state_C, state_n, state_m, page_table, c_prompt, c_sample,
                        w_ada, b_ada, g_pre1, g_post1, g_pre2, g_post2, w_in, b_if, mlstm_norm,
                        lambda_q1, lambda_k1, lambda_q2, lambda_k2, diff_norm, w_out,
                        peer_wq, peer_k1, peer_k2, peer_u, peer_v)
```

```python
import functools
import math

import jax
import jax.numpy as jnp
from jax import lax
from jax.experimental import pallas as pl
from jax.experimental.pallas import tpu as pltpu

F32 = jnp.float32
BF16 = jnp.bfloat16

D_MODEL = 1024
M_HEADS = 4
M_HEAD_DIM = 256
A_HEADS = 8
A_HEAD_DIM = 64
A_VDIM = 128
ROPE_THETA = 10000.0
N_KEYS = 128
PEER_HEADS = 8
PEER_TOPK = 16
NORM_EPS = 1e-6
LAMBDA_INIT = 0.8 - 0.6 * math.exp(-0.3 * 0)
N_MAIN_COLS = 9 * D_MODEL

VT_ROWS = A_VDIM + 16
LANES = 128
NEG = -1e30
VMEM_LIMIT = 56 * 1024 * 1024


def _cparams(sem, flags=None):
    return pltpu.CompilerParams(dimension_semantics=sem, vmem_limit_bytes=VMEM_LIMIT, flags=flags)


def _nt(a, b):
    return lax.dot_general(a, b, (((1,), (1,)), ((), ())), preferred_element_type=F32)


def _tn(a, b):
    return lax.dot_general(a, b, (((0,), (0,)), ((), ())), preferred_element_type=F32)


def _rms(x, g):
    return x * lax.rsqrt(jnp.mean(x * x, axis=-1, keepdims=True) + NORM_EPS) * g


def _split3(x):
    x1 = x.astype(BF16)
    r1 = x - x1.astype(F32)
    x2 = r1.astype(BF16)
    x3 = (r1 - x2.astype(F32)).astype(BF16)
    return x1, x2, x3


def _ada_kernel(c_ref, w_ref, b_ref, o_ref):
    c = c_ref[...]
    s = c * jax.nn.sigmoid(c)
    o_ref[...] = jnp.dot(s.astype(BF16), w_ref[...].astype(BF16), preferred_element_type=F32) + b_ref[...]


def _ada(c_all, w_ada, b_ada):
    r = c_all.shape[0]
    n = w_ada.shape[1]
    bn = D_MODEL
    return pl.pallas_call(
        _ada_kernel,
        grid=(n // bn,),
        in_specs=[pl.BlockSpec((r, D_MODEL), lambda j: (0, 0)),
                  pl.BlockSpec((D_MODEL, bn), lambda j: (0, j)),
                  pl.BlockSpec((1, bn), lambda j: (0, j))],
        out_specs=pl.BlockSpec((r, bn), lambda j: (0, j)),
        out_shape=jax.ShapeDtypeStruct((r, n), F32),
        compiler_params=_cparams(("arbitrary",)),
        name="ada_mod",
    )(c_all, w_ada, b_ada.reshape(1, n))


def _rope128(x, cos, sin_signed):
    fwd = pltpu.roll(x, LANES - 32, axis=1)
    bwd = pltpu.roll(x, 32, axis=1)
    lane = lax.broadcasted_iota(jnp.int32, x.shape, 1)
    rot = jnp.where((lane % 64) < 32, fwd, bwd)
    return x * cos + rot * sin_signed


def _inproj_kernel(x_ref, sc_ref, sh_ref, g_ref, wm_ref, wif_ref, bif_ref, inv_ref,
                   qkv_ref, mo_ref, gab_ref, gates_ref, q1_ref, q2_ref, kf_ref, kb_ref, vf_ref, vb_ref,
                   *, tb, pos0, period, head_major):
    x = x_ref[...]
    h = _rms(x, g_ref[...]) * (1.0 + sc_ref[...]) + sh_ref[...]
    hb = h.astype(BF16)

    def proj(i):
        return jnp.dot(hb, wm_ref[:, i * D_MODEL:(i + 1) * D_MODEL], preferred_element_type=F32)

    qkv_ref[:, 0:D_MODEL] = proj(0).astype(BF16)
    qkv_ref[:, D_MODEL:2 * D_MODEL] = (proj(1) * (M_HEAD_DIM ** -0.5)).astype(BF16)
    qkv_ref[:, 2 * D_MODEL:3 * D_MODEL] = proj(2).astype(BF16)
    mo_ref[...] = proj(3)
    gab_ref[:, 0:D_MODEL] = proj(7)
    gab_ref[:, D_MODEL:2 * D_MODEL] = proj(8)

    zif = jnp.dot(hb, wif_ref[...], preferred_element_type=F32) + bif_ref[...]
    lane = lax.broadcasted_iota(jnp.int32, zif.shape, 1)
    logsig = jnp.minimum(zif, 0.0) - jnp.log1p(jnp.exp(-jnp.abs(zif)))
    gates_ref[...] = jnp.where(lane < M_HEADS, zif, logsig)

    row = lax.broadcasted_iota(jnp.int32, (tb, LANES), 0) + pl.program_id(0) * tb
    pos = (pos0 + row % period).astype(F32)
    ang = pos * inv_ref[...]
    cos = jnp.cos(ang)
    sin = jnp.sin(ang)
    lane = lax.broadcasted_iota(jnp.int32, (tb, LANES), 1)
    sin_signed = jnp.where((lane % 64) < 32, -sin, sin)
    first = lane < A_HEAD_DIM

    dq = proj(4)
    dk = proj(5)
    dv = proj(6)
    vf_ref[...] = dv
    if head_major:
        q_scale = A_HEAD_DIM ** -0.5 * math.log2(math.e)
        extra = lax.broadcasted_iota(jnp.int32, (VT_ROWS - A_VDIM, tb), 0)
        ones_rows = jnp.where(extra == 0, 1.0, 0.0).astype(BF16)
    else:
        q_scale = A_HEAD_DIM ** -0.5
        vb_ref[...] = dv.astype(BF16)
    for hd in range(A_HEADS):
        sl = slice(hd * LANES, (hd + 1) * LANES)
        qr = _rope128(dq[:, sl], cos, sin_signed) * q_scale
        q1 = jnp.where(first, qr, 0.0)
        q2 = jnp.where(first, 0.0, qr)
        kr = _rope128(dk[:, sl], cos, sin_signed)
        kf_ref[:, sl] = kr
        if head_major:
            q1_ref[hd] = q1.T.astype(BF16)
            q2_ref[hd] = q2.T.astype(BF16)
            kb_ref[hd] = kr.astype(BF16)
            vb_ref[hd, 0:A_VDIM, :] = dv[:, sl].T.astype(BF16)
            vb_ref[hd, A_VDIM:VT_ROWS, :] = ones_rows
        else:
            q1_ref[:, sl] = q1.astype(BF16)
            q2_ref[:, sl] = q2.astype(BF16)
            kb_ref[:, sl] = kr.astype(BF16)


def _inproj(x, scale, shift, g_pre, w_main, w_if, b_if128, inv128, *, pos0, period, tb, head_major):
    t = x.shape[0]
    mod_rows = scale.shape[0]
    if mod_rows == 1:
        mod_spec = pl.BlockSpec((1, D_MODEL), lambda i: (0, 0))
    else:
        mod_spec = pl.BlockSpec((tb, D_MODEL), lambda i: (i, 0))
    const = lambda shape: pl.BlockSpec(shape, lambda i: (0, 0))
    tok = lambda w: (pl.BlockSpec((tb, w), lambda i: (i, 0)), (t, w))
    if head_major:
        feat_t = lambda rows: (pl.BlockSpec((A_HEADS, rows, tb), lambda i: (0, 0, i)), (A_HEADS, rows, t))
        tok_h = (pl.BlockSpec((A_HEADS, tb, A_VDIM), lambda i: (0, i, 0)), (A_HEADS, t, A_VDIM))
        q_out, k_out, v_out = feat_t(A_VDIM), tok_h, feat_t(VT_ROWS)
    else:
        q_out = k_out = v_out = tok(D_MODEL)
    outs = [
        (tok(3 * D_MODEL), BF16),
        (tok(D_MODEL), F32),
        (tok(2 * D_MODEL), F32),
        (tok(LANES), F32),
        (q_out, BF16),
        (q_out, BF16),
        (tok(D_MODEL), F32),
        (k_out, BF16),
        (tok(D_MODEL), F32),
        (v_out, BF16),
    ]
    return pl.pallas_call(
        functools.partial(_inproj_kernel, tb=tb, pos0=pos0, period=period, head_major=head_major),
        grid=(t // tb,),
        in_specs=[tok(D_MODEL)[0], mod_spec, mod_spec, const((1, D_MODEL)),
                  const((D_MODEL, N_MAIN_COLS)), const((D_MODEL, LANES)), const((1, LANES)), const((1, LANES))],
        out_specs=[spec for (spec, _), _ in outs],
        out_shape=[jax.ShapeDtypeStruct(shape, dt) for (_, shape), dt in outs],
        compiler_params=_cparams(("parallel",)),
        name="in_proj",
    )(x, scale, shift, g_pre, w_main, w_if, b_if128, inv128)


def _mlstm_kernel(qkv_ref, gates_ref, mo_ref, nw_ref, c0_ref, n0_ref, m0_ref,
                  ya_ref, c_ref, n_ref, m_ref, *, L):
    @pl.when(pl.program_id(1) == 0)
    def _():
        c_ref[...] = c0_ref[...]
        n_ref[...] = n0_ref[...]
        m_ref[...] = m0_ref[...]

    gates = gates_ref[...]
    r = lax.broadcasted_iota(jnp.int32, (L, L), 0)
    c = lax.broadcasted_iota(jnp.int32, (L, L), 1)
    causal = c <= r
    tril = jnp.where(causal, 1.0, 0.0).astype(BF16)
    eye8 = jnp.where(lax.broadcasted_iota(jnp.int32, (8, LANES), 0)
                     == lax.broadcasted_iota(jnp.int32, (8, LANES), 1), 1.0, 0.0).astype(BF16)
    g3 = _split3(gates)
    cums = sum(jnp.dot(tril, gi, preferred_element_type=F32) for gi in g3)
    gt8 = sum(_nt(eye8, gi) for gi in g3)
    bt8 = sum(_nt(eye8, ci) for ci in _split3(cums))

    for h in range(M_HEADS):
        sl = slice(h * M_HEAD_DIM, (h + 1) * M_HEAD_DIM)
        q = qkv_ref[:, h * M_HEAD_DIM:(h + 1) * M_HEAD_DIM]
        k = qkv_ref[:, D_MODEL + h * M_HEAD_DIM:D_MODEL + (h + 1) * M_HEAD_DIM]
        v = qkv_ref[:, 2 * D_MODEL + h * M_HEAD_DIM:2 * D_MODEL + (h + 1) * M_HEAD_DIM]
        ig_col = gates[:, h:h + 1]
        ig_row = gt8[h:h + 1, :]
        b_col = cums[:, M_HEADS + h:M_HEADS + h + 1]
        b_row = bt8[M_HEADS + h:M_HEADS + h + 1, :]
        m_prev = m_ref[0, h:h + 1, 0:1]
        c_old = c_ref[0, h]
        n_old = n_ref[0, h:h + 1, :]

        dmat = jnp.where(causal, b_col - b_row + ig_row, NEG)
        m_inter = b_col + m_prev
        m_t = jnp.maximum(m_inter, jnp.max(dmat, axis=-1, keepdims=True))
        w_inter = jnp.exp(m_inter - m_t)
        a = jnp.exp(dmat - m_t) * _nt(q, k)
        num = w_inter * _nt(q, c_old.astype(BF16)) + jnp.dot(a.astype(BF16), v, preferred_element_type=F32)
        qn = jnp.sum(q.astype(F32) * n_old, axis=-1, keepdims=True)
        den = w_inter * qn + jnp.sum(a, axis=-1, keepdims=True)
        hm = num / jnp.maximum(jnp.abs(den), jnp.exp(-m_t))

        b_last = b_col[L - 1:L, :]
        g_row = b_last - b_row + ig_row
        g_col = b_last - b_col + ig_col
        m_new = jnp.maximum(b_last + m_prev, jnp.max(g_row, axis=-1, keepdims=True))
        dec = jnp.exp(b_last + m_prev - m_new)
        w_col = jnp.exp(g_col - m_new)
        vw = (v.astype(F32) * w_col).astype(BF16)
        c_ref[0, h] = dec * c_old + _tn(vw, k)
        n_ref[0, h:h + 1, :] = dec * n_old + jnp.sum(k.astype(F32) * w_col, axis=0, keepdims=True)
        m_ref[0, h:h + 1, :] = jnp.broadcast_to(m_new, (1, LANES))

        ya_ref[:, sl] = _rms(hm, nw_ref[:, sl]) * jax.nn.sigmoid(mo_ref[:, sl])


def _mlstm(qkv, gates, mo, norm_w, c0, n0, m0, *, batch, L):
    t_total = qkv.shape[0]
    nc = t_total // (batch * L)
    tok = lambda w: pl.BlockSpec((L, w), lambda b, c: (b * nc + c, 0))
    st4 = pl.BlockSpec((1, M_HEADS, M_HEAD_DIM, M_HEAD_DIM), lambda b, c: (b, 0, 0, 0))
    st3n = pl.BlockSpec((1, M_HEADS, M_HEAD_DIM), lambda b, c: (b, 0, 0))
    st3m = pl.BlockSpec((1, M_HEADS, LANES), lambda b, c: (b, 0, 0))
    return pl.pallas_call(
        functools.partial(_mlstm_kernel, L=L),
        grid=(batch, nc),
        in_specs=[tok(3 * D_MODEL), tok(LANES), tok(D_MODEL), pl.BlockSpec((1, D_MODEL), lambda b, c: (0, 0)),
                  st4, st3n, st3m],
        out_specs=[tok(D_MODEL), st4, st3n, st3m],
        out_shape=[jax.ShapeDtypeStruct((t_total, D_MODEL), F32),
                   jax.ShapeDtypeStruct(c0.shape, F32),
                   jax.ShapeDtypeStruct(n0.shape, F32),
                   jax.ShapeDtypeStruct(m0.shape, F32)],
        compiler_params=_cparams(("parallel", "arbitrary")),
        name="mlstm",
    )(qkv, gates, mo, norm_w, c0, n0, m0)


def _lambda_full(lq1_ref, lk1_ref, lq2_ref, lk2_ref):
    a = jnp.exp(jnp.sum(lq1_ref[...] * lk1_ref[...], axis=-1, keepdims=True))
    b = jnp.exp(jnp.sum(lq2_ref[...] * lk2_ref[...], axis=-1, keepdims=True))
    return a - b + LAMBDA_INIT


def _attn_prompt_kernel(qi_ref, kj_ref, q1_ref, q2_ref, k_ref, vt_ref, lq1_ref, lk1_ref, lq2_ref, lk2_ref, dn_ref,
                        o_ref, m_sc, a_sc, acc_sc, s_sc, p_sc, *, tq):
    step = pl.program_id(0)
    qi = qi_ref[step]
    kj = kj_ref[step]
    last_kj = qi // 2
    q_refs = (q1_ref, q2_ref)

    @pl.when(kj == 0)
    def _():
        m_sc[...] = jnp.full(m_sc.shape, NEG, F32)
        acc_sc[...] = jnp.zeros(acc_sc.shape, F32)

    def run(subs):
        n = 2 * len(subs)

        def scores(head, g):
            sub = subs[g // 2][0]
            s_sc[g] = jnp.dot(k_ref[head, sub * tq:(sub + 1) * tq, :], q_refs[g % 2][head],
                              preferred_element_type=F32)

        def softmax(head, g):
            pair = 2 * head + g % 2
            s = s_sc[g]
            if subs[g // 2][1]:
                key = lax.broadcasted_iota(jnp.int32, (tq, tq), 0)
                qry = lax.broadcasted_iota(jnp.int32, (tq, tq), 1)
                s = jnp.where(key <= qry, s, NEG)
            m_prev = m_sc[pair]
            m_new = jnp.maximum(m_prev, jnp.max(s, axis=0, keepdims=True))
            a_sc[g] = jnp.exp2(m_prev - m_new)
            p_sc[g] = jnp.exp2(s - m_new).astype(BF16)
            m_sc[pair] = m_new

        def values(head, g):
            pair = 2 * head + g % 2
            sub = subs[g // 2][0]
            pv = jnp.dot(vt_ref[head, :, sub * tq:(sub + 1) * tq], p_sc[g], preferred_element_type=F32)
            acc_sc[pair] = a_sc[g] * acc_sc[pair] + pv

        for g in range(n):
            scores(0, g)
        softmax(0, 0)

        def body(head, carry):
            for g in range(n):
                values(head, g)
                if g + 1 < n:
                    softmax(head, g + 1)
                else:
                    softmax(head + 1, 0)
                scores(head + 1, g)
            return carry

        lax.fori_loop(0, A_HEADS - 1, body, 0)
        for g in range(n):
            values(A_HEADS - 1, g)
            if g + 1 < n:
                softmax(A_HEADS - 1, g + 1)

    @pl.when(kj < last_kj)
    def _():
        run(((0, False), (1, False)))

    @pl.when((kj == last_kj) & (qi % 2 == 1))
    def _():
        run(((0, False), (1, True)))

    @pl.when((kj == last_kj) & (qi % 2 == 0))
    def _():
        run(((0, True),))

    @pl.when(kj == last_kj)
    def _():
        lam = _lambda_full(lq1_ref, lk1_ref, lq2_ref, lk2_ref)
        for h in range(A_HEADS):
            a1 = acc_sc[2 * h]
            a2 = acc_sc[2 * h + 1]
            ot = (a1[0:A_VDIM] / a1[A_VDIM:A_VDIM + 1]
                  - lam * (a2[0:A_VDIM] / a2[A_VDIM:A_VDIM + 1]))
            o_ref[:, h * LANES:(h + 1) * LANES] = _rms(ot.T, dn_ref[...]) * (1.0 - LAMBDA_INIT)


def _attn_prompt(q1t, q2t, kh, vt, lams, diff_norm, *, tq):
    t = kh.shape[1]
    nq = t // tq
    tk = 2 * tq
    steps = [(qi, kj) for qi in range(nq) for kj in range(qi // 2 + 1)]
    qi_of = jnp.asarray([s[0] for s in steps], jnp.int32)
    kj_of = jnp.asarray([s[1] for s in steps], jnp.int32)
    qspec = pl.BlockSpec((A_HEADS, A_VDIM, tq), lambda s, qi, kj: (0, 0, qi[s]))
    kspec = pl.BlockSpec((A_HEADS, tk, A_VDIM), lambda s, qi, kj: (0, kj[s], 0))
    vspec = pl.BlockSpec((A_HEADS, VT_ROWS, tk), lambda s, qi, kj: (0, 0, kj[s]))
    small = lambda n: pl.BlockSpec((1, n), lambda s, qi, kj: (0, 0))
    grid_spec = pltpu.PrefetchScalarGridSpec(
        num_scalar_prefetch=2,
        grid=(len(steps),),
        in_specs=[qspec, qspec, kspec, vspec] + [small(A_HEAD_DIM)] * 4 + [small(A_VDIM)],
        out_specs=pl.BlockSpec((tq, D_MODEL), lambda s, qi, kj: (qi[s], 0)),
        scratch_shapes=[pltpu.VMEM((2 * A_HEADS, 1, tq), F32),
                        pltpu.VMEM((4, 1, tq), F32),
                        pltpu.VMEM((2 * A_HEADS, VT_ROWS, tq), F32),
                        pltpu.VMEM((4, tq, tq), F32), pltpu.VMEM((4, tq, tq), BF16)],
    )
    return pl.pallas_call(
        functools.partial(_attn_prompt_kernel, tq=tq),
        grid_spec=grid_spec,
        out_shape=jax.ShapeDtypeStruct((t, D_MODEL), F32),
        compiler_params=_cparams(("arbitrary",)),
        name="diff_attn_prompt",
    )(qi_of, kj_of, q1t, q2t, kh, vt, *lams, diff_norm)


def _attn_sample_kernel(pt_ref, q_ref, kn_ref, vn_ref, lq1_ref, lk1_ref, lq2_ref, lk2_ref, dn_ref, *rest,
                        ppb, n_new, page):
    k_refs = rest[:ppb]
    v_refs = rest[ppb:2 * ppb]
    o_ref, m_sc, l_sc, acc_sc = rest[2 * ppb:]
    j = pl.program_id(1)
    rows = q_ref.shape[2]

    @pl.when(j == 0)
    def _():
        m_sc[...] = jnp.full(m_sc.shape, NEG, F32)
        l_sc[...] = jnp.zeros(l_sc.shape, F32)
        acc_sc[...] = jnp.zeros(acc_sc.shape, F32)

    def update(scores, values):
        s = jnp.concatenate([scores(h) for h in range(A_HEADS)], axis=0)
        m_prev = m_sc[...]
        m_new = jnp.maximum(m_prev, jnp.max(s, axis=-1, keepdims=True))
        alpha = jnp.exp(m_prev - m_new)
        p = jnp.exp(s - m_new)
        l_sc[...] = alpha * l_sc[...] + jnp.sum(p, axis=-1, keepdims=True)
        pb = p.astype(BF16)
        pv = jnp.concatenate([jnp.dot(pb[h * rows:(h + 1) * rows], values(h), preferred_element_type=F32)
                              for h in range(A_HEADS)], axis=0)
        acc_sc[...] = alpha * acc_sc[...] + pv
        m_sc[...] = m_new

    def head_rows(refs, h):
        parts = [r[0, pl.ds(h, page, stride=A_HEADS), :] for r in refs]
        return jnp.concatenate(parts, axis=0).astype(BF16)

    update(lambda h: _nt(q_ref[0, h], head_rows(k_refs, h)), lambda h: head_rows(v_refs, h))

    @pl.when(j == pl.num_programs(1) - 1)
    def _():
        lam = _lambda_full(lq1_ref, lk1_ref, lq2_ref, lk2_ref)
        r = lax.broadcasted_iota(jnp.int32, (rows, rows), 0)
        c = lax.broadcasted_iota(jnp.int32, (rows, rows), 1)
        keep = (c <= r % n_new) & (c < n_new)
        update(lambda h: jnp.where(keep, _nt(q_ref[0, h], kn_ref[0, h]), NEG), lambda h: vn_ref[0, h])
        o = acc_sc[...] / l_sc[...]
        for h in range(A_HEADS):
            od = o[h * rows:h * rows + n_new, :] - lam * o[h * rows + n_new:h * rows + 2 * n_new, :]
            o_ref[0, :, h * LANES:(h + 1) * LANES] = _rms(od, dn_ref[...]) * (1.0 - LAMBDA_INIT)


def _attn_sample(q_rows, k_new, v_new, cache_k, cache_v, page_table, lams, diff_norm, *, ppb, n_new, page_base):
    bsz, n_pages = page_table.shape
    page = cache_k.shape[1] // A_HEADS
    rows = q_rows.shape[2]
    seq = lambda: pl.BlockSpec((1, A_HEADS, rows, A_VDIM), lambda b, j, pt: (b, 0, 0, 0))
    small = lambda n: pl.BlockSpec((1, n), lambda b, j, pt: (0, 0))

    def page_spec(i):
        return pl.BlockSpec((1, page * A_HEADS, A_VDIM), lambda b, j, pt: (page_base + pt[b, j * ppb + i], 0, 0))

    grid_spec = pltpu.PrefetchScalarGridSpec(
        num_scalar_prefetch=1,
        grid=(bsz, n_pages // ppb),
        in_specs=[seq(), seq(), seq()] + [small(A_HEAD_DIM)] * 4 + [small(A_VDIM)]
                 + [page_spec(i) for i in range(ppb)] + [page_spec(i) for i in range(ppb)],
        out_specs=pl.BlockSpec((1, n_new, D_MODEL), lambda b, j, pt: (b, 0, 0)),
        scratch_shapes=[pltpu.VMEM((A_HEADS * rows, 1), F32),
                        pltpu.VMEM((A_HEADS * rows, 1), F32),
                        pltpu.VMEM((A_HEADS * rows, A_VDIM), F32)],
    )
    return pl.pallas_call(
        functools.partial(_attn_sample_kernel, ppb=ppb, n_new=n_new, page=page),
        grid_spec=grid_spec,
        out_shape=jax.ShapeDtypeStruct((bsz, n_new, D_MODEL), F32),
        compiler_params=_cparams(("parallel", "arbitrary")),
        name="diff_attn_sample",
    )(page_table, q_rows, k_new, v_new, *lams, diff_norm, *([cache_k] * ppb), *([cache_v] * ppb))


def _postmix_kernel(x_ref, ya_ref, yb_ref, gab_ref, g1_ref, sc2_ref, sh2_ref, gpost_ref, gpre_ref, wo_ref, wq_ref,
                    x1_ref, h2_ref, qp_ref):
    merged = (jax.nn.sigmoid(gab_ref[:, 0:D_MODEL]) * ya_ref[...]
              + jax.nn.sigmoid(gab_ref[:, D_MODEL:2 * D_MODEL]) * yb_ref[...])
    o = jnp.dot(merged.astype(BF16), wo_ref[...], preferred_element_type=F32)
    x1 = x_ref[...] + g1_ref[...] * _rms(o, gpost_ref[...])
    x1_ref[...] = x1
    h2 = (_rms(x1, gpre_ref[...]) * (1.0 + sc2_ref[...]) + sh2_ref[...]).astype(BF16)
    h2_ref[...] = h2
    qp_ref[...] = jnp.dot(h2, wq_ref[...], preferred_element_type=F32).astype(BF16)


def _postmix(x, ya, yb, gab, gate1, scale2, shift2, g_post1, g_pre2, w_out, w_q, *, tb):
    t = x.shape[0]
    nq = w_q.shape[1]
    if gate1.shape[0] == 1:
        mod_spec = pl.BlockSpec((1, D_MODEL), lambda i: (0, 0))
    else:
        mod_spec = pl.BlockSpec((tb, D_MODEL), lambda i: (i, 0))
    const = lambda shape: pl.BlockSpec(shape, lambda i: (0, 0))
    tok = lambda w: pl.BlockSpec((tb, w), lambda i: (i, 0))
    return pl.pallas_call(
        _postmix_kernel,
        grid=(t // tb,),
        in_specs=[tok(D_MODEL), tok(D_MODEL), tok(D_MODEL), tok(2 * D_MODEL), mod_spec, mod_spec, mod_spec,
                  const((1, D_MODEL)), const((1, D_MODEL)), const((D_MODEL, D_MODEL)), const((D_MODEL, nq))],
        out_specs=[tok(D_MODEL), tok(D_MODEL), tok(nq)],
        out_shape=[jax.ShapeDtypeStruct((t, D_MODEL), F32),
                   jax.ShapeDtypeStruct((t, D_MODEL), BF16),
                   jax.ShapeDtypeStruct((t, nq), BF16)],
        compiler_params=_cparams(("parallel",)),
        name="post_mix",
    )(x, ya, yb, gab, gate1, scale2, shift2, g_post1, g_pre2, w_out, w_q)


NOT_TOP = 255.0


def _top_ranked(s, n):
    vals = []
    work = s
    rank = jnp.full(s.shape, NOT_TOP, F32)
    for i in range(n):
        mx = jnp.max(work, axis=0, keepdims=True)
        hit = work == mx
        rank = jnp.where(hit, float(i), rank)
        work = jnp.where(hit, NEG, work)
        vals.append(mx)
    return vals, rank


def _peer_score_kernel(qp_ref, k1_ref, k2_ref, c1_ref, e1_ref, r2_ref, e2_ref, *, tt):
    ab = N_KEYS // 8
    jrow = lax.broadcasted_iota(jnp.int32, (8, tt), 0)
    for h in range(PEER_HEADS):
        base = h * 2 * N_KEYS
        s1 = _nt(k1_ref[h], qp_ref[:, base:base + N_KEYS])
        s2 = _nt(k2_ref[h], qp_ref[:, base + N_KEYS:base + 2 * N_KEYS])
        t1, rank1 = _top_ranked(s1, PEER_TOPK)
        t2, rank2 = _top_ranked(s2, PEER_TOPK)
        t2a = jnp.concatenate(t2[:8], axis=0)
        t2b = jnp.concatenate(t2[8:], axis=0)
        slabs = [t1[0] + t2a, t1[0] + t2b]
        for i in range(1, PEER_TOPK):
            slabs.append(jnp.where(jrow < PEER_TOPK // (i + 1), t1[i] + t2a, NEG))
        cand = jnp.concatenate(slabs, axis=0)
        tau = _top_ranked(cand, PEER_TOPK)[0][-1]
        top = t1[0] + t2[0]
        sel = cand >= tau
        z = jnp.sum(jnp.where(sel, jnp.exp(cand - top), 0.0), axis=0, keepdims=True)
        ones = jnp.where(sel, 1.0, 0.0)
        counts = [jnp.sum(ones[0:16], axis=0, keepdims=True)]
        for i in range(1, PEER_TOPK):
            counts.append(jnp.sum(ones[8 + 8 * i:16 + 8 * i], axis=0, keepdims=True))
        c1 = jnp.zeros(s1.shape, F32)
        for i in range(PEER_TOPK):
            c1 = jnp.where(rank1 == float(i), counts[i], c1)
        c1_ref[:, h] = c1.reshape(ab, 8, tt)
        e1_ref[:, h] = (jnp.exp(s1 - t1[0]) * (0.5 / z)).reshape(ab, 8, tt)
        r2_ref[h] = rank2.astype(BF16)
        e2_ref[h] = jnp.exp(s2 - t2[0]).astype(BF16)


def _peer_scores(qp, k1, k2, *, tt):
    t = qp.shape[0]
    ab = N_KEYS // 8
    a_spec = pl.BlockSpec((ab, PEER_HEADS, 8, tt), lambda i: (0, 0, 0, i))
    b_spec = pl.BlockSpec((PEER_HEADS, N_KEYS, tt), lambda i: (0, 0, i))
    kspec = pl.BlockSpec((PEER_HEADS, N_KEYS, N_KEYS), lambda i: (0, 0, 0))
    a_shape = jax.ShapeDtypeStruct((ab, PEER_HEADS, 8, t), F32)
    b_shape = jax.ShapeDtypeStruct((PEER_HEADS, N_KEYS, t), BF16)
    return pl.pallas_call(
        functools.partial(_peer_score_kernel, tt=tt),
        grid=(t // tt,),
        in_specs=[pl.BlockSpec((tt, qp.shape[1]), lambda i: (i, 0)), kspec, kspec],
        out_specs=[a_spec, a_spec, b_spec, b_spec],
        out_shape=[a_shape, a_shape, b_shape, b_shape],
        compiler_params=_cparams(("parallel",)),
        name="peer_scores",
    )(qp, k1, k2)


def _gelu_x2(x):
    return x * (1.0 + lax.erf(x * (2.0 ** -0.5)))


def _peer_dense_kernel(h2_ref, u_ref, vt_ref, c1_ref, e1_ref, r2_ref, e2_ref, x1_ref, g2_ref, gpost_ref,
                       y_ref, acc_sc, a_sc, *, a_per):
    j = pl.program_id(1)

    @pl.when(j == 0)
    def _():
        acc_sc[...] = jnp.zeros(acc_sc.shape, F32)

    zt = _nt(u_ref[...], h2_ref[...])
    zero = jnp.zeros((), BF16)
    tt = h2_ref.shape[0]

    def key_rows(ref, al, h):
        row = ref[al // 8, h, al % 8:al % 8 + 1, :]
        tile = jnp.broadcast_to(row, (16, tt)).astype(BF16)
        return jnp.tile(tile, (N_KEYS // 16, 1))

    for al in range(a_per):
        w = None
        for h in range(PEER_HEADS):
            term = jnp.where(r2_ref[h] < key_rows(c1_ref, al, h), e2_ref[h] * key_rows(e1_ref, al, h), zero)
            w = term if w is None else w + term
        rows = slice(al * N_KEYS, (al + 1) * N_KEYS)
        a_sc[rows, :] = w * _gelu_x2(zt[rows, :]).astype(BF16)
    acc_sc[...] += jnp.dot(vt_ref[...], a_sc[...], preferred_element_type=F32)

    @pl.when(j == pl.num_programs(1) - 1)
    def _():
        f = acc_sc[...].T
        y_ref[...] = x1_ref[...] + g2_ref[...] * _rms(f, gpost_ref[...])


def _peer_dense(h2, u_b, vt_b, c1, e1, r2, e2, x1, gate2, g_post2, *, tt, a_per):
    t = h2.shape[0]
    n_exp = u_b.shape[0]
    eb = a_per * N_KEYS
    if gate2.shape[0] == 1:
        mod_spec = pl.BlockSpec((1, D_MODEL), lambda i, j: (0, 0))
    else:
        mod_spec = pl.BlockSpec((tt, D_MODEL), lambda i, j: (i, 0))
    a_spec = pl.BlockSpec((a_per // 8, PEER_HEADS, 8, tt), lambda i, j: (j, 0, 0, i))
    b_spec = pl.BlockSpec((PEER_HEADS, N_KEYS, tt), lambda i, j: (0, 0, i))
    return pl.pallas_call(
        functools.partial(_peer_dense_kernel, a_per=a_per),
        grid=(t // tt, n_exp // eb),
        in_specs=[pl.BlockSpec((tt, D_MODEL), lambda i, j: (i, 0)),
                  pl.BlockSpec((eb, D_MODEL), lambda i, j: (j, 0)),
                  pl.BlockSpec((D_MODEL, eb), lambda i, j: (0, j)),
                  a_spec, a_spec, b_spec, b_spec,
                  pl.BlockSpec((tt, D_MODEL), lambda i, j: (i, 0)),
                  mod_spec,
                  pl.BlockSpec((1, D_MODEL), lambda i, j: (0, 0))],
        out_specs=pl.BlockSpec((tt, D_MODEL), lambda i, j: (i, 0)),
        out_shape=jax.ShapeDtypeStruct((t, D_MODEL), F32),
        scratch_shapes=[pltpu.VMEM((D_MODEL, tt), F32), pltpu.VMEM((eb, tt), BF16)],
        compiler_params=_cparams(("parallel", "arbitrary")),
        name="peer_dense",
    )(h2, u_b, vt_b, c1, e1, r2, e2, x1, gate2, g_post2)


def _block(t, pref):
    return pref if t % pref == 0 else t


def _layer(x, mods, weights, *, pos0, period, run_mlstm, attend, head_major):
    shift1, scale1, gate1, shift2, scale2, gate2 = mods
    w = weights
    t = x.shape[0]
    tb = _block(t, 256)
    qkv, mo, gab, gates, q1, q2, kf, kb, vf, vb = _inproj(
        x, scale1, shift1, w["g_pre1"], w["w_main"], w["w_if"], w["b_if128"], w["inv128"],
        pos0=pos0, period=period, tb=tb, head_major=head_major)

    ya, c_new, n_new, m_new = run_mlstm(qkv, gates, mo)
    yb = attend(q1, q2, kb, vb)

    x1, h2, qp = _postmix(x, ya, yb, gab, gate1, scale2, shift2, w["g_post1"], w["g_pre2"], w["w_out"], w["w_q"], tb=tb)
    tt = _block(t, 128)
    c1, e1, r2, e2 = _peer_scores(qp, w["k1"], w["k2"], tt=tt)
    ttd = _block(t, 512)
    y = _peer_dense(h2, w["u"], w["vt"], c1, e1, r2, e2, x1, gate2, w["g_post2"], tt=ttd, a_per=8)
    return y, kf, vf, c_new, n_new, m_new


def _hybrid_step(x_prompt, x_sample, cache_k, cache_v, state_C, state_n, state_m, page_table, c_prompt, c_sample,
                 w_ada, b_ada, g_pre1, g_post1, g_pre2, g_post2, w_in, b_if, mlstm_norm,
                 lambda_q1, lambda_k1, lambda_q2, lambda_k2, diff_norm, w_out,
                 peer_wq, peer_k1, peer_k2, peer_u, peer_v, *, m_chunk=128, tq=512, ppb=16):
    l = 0
    bp, sp, _ = x_prompt.shape
    bs, ts, _ = x_sample.shape
    assert bp == 1
    row = lambda a: a.reshape(1, -1)

    wi = w_in[l]
    n_if = 2 * M_HEADS
    w_main = jnp.concatenate([wi[:, :4 * D_MODEL], wi[:, 4 * D_MODEL + n_if:]], axis=1).astype(BF16)
    w_if = jnp.pad(wi[:, 4 * D_MODEL:4 * D_MODEL + n_if], ((0, 0), (0, LANES - n_if))).astype(BF16)
    b_if128 = jnp.pad(b_if[l], (0, LANES - n_if)).reshape(1, LANES)
    inv = ROPE_THETA ** (-jnp.arange(0, A_HEAD_DIM, 2, dtype=F32) / A_HEAD_DIM)
    inv128 = jnp.tile(inv, LANES // inv.shape[0]).reshape(1, LANES)
    weights = dict(
        g_pre1=row(g_pre1[l]), g_post1=row(g_post1[l]), g_pre2=row(g_pre2[l]), g_post2=row(g_post2[l]),
        w_main=w_main, w_if=w_if, b_if128=b_if128, inv128=inv128,
        w_out=w_out[l].astype(BF16), w_q=peer_wq[l].astype(BF16),
        k1=peer_k1[l].astype(BF16), k2=peer_k2[l].astype(BF16),
        u=peer_u[l].astype(BF16), vt=peer_v[l].T.astype(BF16))
    lams = [row(a[l]) for a in (lambda_q1, lambda_k1, lambda_q2, lambda_k2)]
    dn = row(diff_norm[l])
    nw = row(mlstm_norm[l])

    n_c = bp + bs
    c_all = jnp.pad(jnp.concatenate([c_prompt, c_sample], axis=0), ((0, (-n_c) % 8), (0, 0)))
    mod = _ada(c_all, w_ada[l], b_ada[l])
    mods_p = [mod[0:1, i * D_MODEL:(i + 1) * D_MODEL] for i in range(6)]
    mods_s = [jnp.repeat(mod[bp:n_c, i * D_MODEL:(i + 1) * D_MODEL], ts, axis=0) for i in range(6)]

    def run_mlstm_p(qkv, gates, mo):
        c0 = jnp.zeros((bp, M_HEADS, M_HEAD_DIM, M_HEAD_DIM), F32)
        n0 = jnp.zeros((bp, M_HEADS, M_HEAD_DIM), F32)
        m0 = jnp.zeros((bp, M_HEADS, LANES), F32)
        return _mlstm(qkv, gates, mo, nw, c0, n0, m0, batch=bp, L=math.gcd(sp, m_chunk))

    def attend_p(q1t, q2t, kh, vt):
        return _attn_prompt(q1t, q2t, kh, vt, lams, dn, tq=_block(sp, tq))

    yp, kp, vp, cp, n_p, mp = _layer(
        x_prompt.reshape(sp, D_MODEL), mods_p, weights, pos0=0, period=sp,
        run_mlstm=run_mlstm_p, attend=attend_p, head_major=True)

    lpad = 16
    past_len = page_table.shape[1] * cache_k.shape[2]

    def pad_tokens(a, fill_row=None):
        a3 = a.reshape(bs, ts, a.shape[-1])
        if fill_row is None:
            out = jnp.pad(a3, ((0, 0), (0, lpad - ts), (0, 0)))
        else:
            fill = jnp.broadcast_to(fill_row.astype(a.dtype), (bs, lpad - ts, a.shape[-1]))
            out = jnp.concatenate([a3, fill], axis=1)
        return out.reshape(bs * lpad, a.shape[-1])

    def run_mlstm_s(qkv, gates, mo):
        lane = jnp.arange(LANES)
        gate_fill = jnp.where(lane < M_HEADS, NEG, 0.0).astype(F32)
        m0 = jnp.broadcast_to(state_m[l][:, :, None], (bs, M_HEADS, LANES))
        ya, c_new, n_new, m_new = _mlstm(pad_tokens(qkv), pad_tokens(gates, gate_fill), pad_tokens(mo), nw,
                                         state_C[l], state_n[l], m0, batch=bs, L=lpad)
        ya = ya.reshape(bs, lpad, D_MODEL)[:, :ts].reshape(bs * ts, D_MODEL)
        return ya, c_new, n_new, m_new

    def attend_s(q1, q2, kb, vb):
        heads = lambda a: a.reshape(bs, ts, A_HEADS, A_VDIM).transpose(0, 2, 1, 3)
        zeros = lambda n: jnp.zeros((bs, A_HEADS, n, A_VDIM), BF16)
        q_rows = jnp.concatenate([heads(q1), heads(q2), zeros(lpad - 2 * ts)], axis=2)
        k_new = jnp.concatenate([heads(kb), zeros(lpad - ts)], axis=2)
        v_new = jnp.concatenate([heads(vb), zeros(lpad - ts)], axis=2)
        depth, n_pool, page = cache_k.shape[0], cache_k.shape[1], cache_k.shape[2]
        ck = cache_k.reshape(depth * n_pool, page * A_HEADS, A_VDIM)
        cv = cache_v.reshape(depth * n_pool, page * A_HEADS, A_VDIM)
        o = _attn_sample(q_rows, k_new, v_new, ck, cv, page_table, lams, dn,
                         ppb=math.gcd(page_table.shape[1], ppb), n_new=ts, page_base=l * n_pool)
        return o.reshape(bs * ts, D_MODEL)

    ys, k_s, v_s, cs, n_s, ms = _layer(
        x_sample.reshape(bs * ts, D_MODEL), mods_s, weights, pos0=past_len, period=ts,
        run_mlstm=run_mlstm_s, attend=attend_s, head_major=False)

    return (yp.reshape(bp, sp, D_MODEL), ys.reshape(bs, ts, D_MODEL),
            kp.reshape(1, bp, sp, A_HEADS, A_VDIM), vp.reshape(1, bp, sp, A_HEADS, A_VDIM),
            cp[None], n_p[None], mp[None, :, :, 0],
            k_s.reshape(1, bs, ts, A_HEADS, A_VDIM), v_s.reshape(1, bs, ts, A_HEADS, A_VDIM),
            cs[None], n_s[None], ms[None, :, :, 0])


def kernel(x_prompt, x_sample, cache_k, cache_v, state_C, state_n, state_m, page_table, c_prompt, c_sample, w_ada, b_ada, g_pre1, g_post1, g_pre2, g_post2, w_in, b_if, mlstm_norm, lambda_q1, lambda_k1, lambda_q2, lambda_k2, diff_norm, w_out, peer_wq, peer_k1, peer_k2, peer_u, peer_v):
    return _hybrid_step(x_prompt, x_sample, cache_k, cache_v, state_C, state_n, state_m, page_table, c_prompt, c_sample,
                        w_ada, b_ada, g_pre1, g_post1, g_pre2, g_post2, w_in, b_if, mlstm_norm,
                        lambda_q1, lambda_k1, lambda_q2, lambda_k2, diff_norm, w_out,
                        peer_wq, peer_k1, peer_k2, peer_u, peer_v)
```

```python
import functools
import math

import jax
import jax.numpy as jnp
from jax import lax
from jax.experimental import pallas as pl
from jax.experimental.pallas import tpu as pltpu

F32 = jnp.float32
BF16 = jnp.bfloat16

D_MODEL = 1024
M_HEADS = 4
M_HEAD_DIM = 256
A_HEADS = 8
A_HEAD_DIM = 64
A_VDIM = 128
ROPE_THETA = 10000.0
N_KEYS = 128
PEER_HEADS = 8
PEER_TOPK = 16
NORM_EPS = 1e-6
LAMBDA_INIT = 0.8 - 0.6 * math.exp(-0.3 * 0)
N_MAIN_COLS = 9 * D_MODEL

VT_ROWS = A_VDIM + 16
LANES = 128
NEG = -1e30
VMEM_LIMIT = 56 * 1024 * 1024


def _cparams(sem, flags=None):
    return pltpu.CompilerParams(dimension_semantics=sem, vmem_limit_bytes=VMEM_LIMIT, flags=flags)


def _nt(a, b):
    return lax.dot_general(a, b, (((1,), (1,)), ((), ())), preferred_element_type=F32)


def _tn(a, b):
    return lax.dot_general(a, b, (((0,), (0,)), ((), ())), preferred_element_type=F32)


def _rms(x, g):
    return x * lax.rsqrt(jnp.mean(x * x, axis=-1, keepdims=True) + NORM_EPS) * g


def _split3(x):
    x1 = x.astype(BF16)
    r1 = x - x1.astype(F32)
    x2 = r1.astype(BF16)
    x3 = (r1 - x2.astype(F32)).astype(BF16)
    return x1, x2, x3


def _ada_kernel(c_ref, w_ref, b_ref, o_ref):
    c = c_ref[...]
    s = c * jax.nn.sigmoid(c)
    o_ref[...] = jnp.dot(s.astype(BF16), w_ref[...].astype(BF16), preferred_element_type=F32) + b_ref[...]


def _ada(c_all, w_ada, b_ada):
    r = c_all.shape[0]
    n = w_ada.shape[1]
    bn = D_MODEL
    return pl.pallas_call(
        _ada_kernel,
        grid=(n // bn,),
        in_specs=[pl.BlockSpec((r, D_MODEL), lambda j: (0, 0)),
                  pl.BlockSpec((D_MODEL, bn), lambda j: (0, j)),
                  pl.BlockSpec((1, bn), lambda j: (0, j))],
        out_specs=pl.BlockSpec((r, bn), lambda j: (0, j)),
        out_shape=jax.ShapeDtypeStruct((r, n), F32),
        compiler_params=_cparams(("arbitrary",)),
        name="ada_mod",
    )(c_all, w_ada, b_ada.reshape(1, n))


def _rope128(x, cos, sin_signed):
    fwd = pltpu.roll(x, LANES - 32, axis=1)
    bwd = pltpu.roll(x, 32, axis=1)
    lane = lax.broadcasted_iota(jnp.int32, x.shape, 1)
    rot = jnp.where((lane % 64) < 32, fwd, bwd)
    return x * cos + rot * sin_signed


def _inproj_kernel(x_ref, sc_ref, sh_ref, g_ref, wm_ref, wif_ref, bif_ref, inv_ref,
                   qkv_ref, mo_ref, gab_ref, gates_ref, q1_ref, q2_ref, kf_ref, kb_ref, vf_ref, vb_ref,
                   *, tb, pos0, period, head_major):
    x = x_ref[...]
    h = _rms(x, g_ref[...]) * (1.0 + sc_ref[...]) + sh_ref[...]
    hb = h.astype(BF16)

    def proj(i):
        return jnp.dot(hb, wm_ref[:, i * D_MODEL:(i + 1) * D_MODEL], preferred_element_type=F32)

    qkv_ref[:, 0:D_MODEL] = proj(0).astype(BF16)
    qkv_ref[:, D_MODEL:2 * D_MODEL] = (proj(1) * (M_HEAD_DIM ** -0.5)).astype(BF16)
    qkv_ref[:, 2 * D_MODEL:3 * D_MODEL] = proj(2).astype(BF16)
    mo_ref[...] = proj(3)
    gab_ref[:, 0:D_MODEL] = proj(7)
    gab_ref[:, D_MODEL:2 * D_MODEL] = proj(8)

    zif = jnp.dot(hb, wif_ref[...], preferred_element_type=F32) + bif_ref[...]
    lane = lax.broadcasted_iota(jnp.int32, zif.shape, 1)
    logsig = jnp.minimum(zif, 0.0) - jnp.log1p(jnp.exp(-jnp.abs(zif)))
    gates_ref[...] = jnp.where(lane < M_HEADS, zif, logsig)

    row = lax.broadcasted_iota(jnp.int32, (tb, LANES), 0) + pl.program_id(0) * tb
    pos = (pos0 + row % period).astype(F32)
    ang = pos * inv_ref[...]
    cos = jnp.cos(ang)
    sin = jnp.sin(ang)
    lane = lax.broadcasted_iota(jnp.int32, (tb, LANES), 1)
    sin_signed = jnp.where((lane % 64) < 32, -sin, sin)
    first = lane < A_HEAD_DIM

    dq = proj(4)
    dk = proj(5)
    dv = proj(6)
    vf_ref[...] = dv
    if head_major:
        q_scale = A_HEAD_DIM ** -0.5 * math.log2(math.e)
        extra = lax.broadcasted_iota(jnp.int32, (VT_ROWS - A_VDIM, tb), 0)
        ones_rows = jnp.where(extra == 0, 1.0, 0.0).astype(BF16)
    else:
        q_scale = A_HEAD_DIM ** -0.5
        vb_ref[...] = dv.astype(BF16)
    for hd in range(A_HEADS):
        sl = slice(hd * LANES, (hd + 1) * LANES)
        qr = _rope128(dq[:, sl], cos, sin_signed) * q_scale
        q1 = jnp.where(first, qr, 0.0)
        q2 = jnp.where(first, 0.0, qr)
        kr = _rope128(dk[:, sl], cos, sin_signed)
        kf_ref[:, sl] = kr
        if head_major:
            q1_ref[hd] = q1.T.astype(BF16)
            q2_ref[hd] = q2.T.astype(BF16)
            kb_ref[hd] = kr.astype(BF16)
            vb_ref[hd, 0:A_VDIM, :] = dv[:, sl].T.astype(BF16)
            vb_ref[hd, A_VDIM:VT_ROWS, :] = ones_rows
        else:
            q1_ref[:, sl] = q1.astype(BF16)
            q2_ref[:, sl] = q2.astype(BF16)
            kb_ref[:, sl] = kr.astype(BF16)


def _inproj(x, scale, shift, g_pre, w_main, w_if, b_if128, inv128, *, pos0, period, tb, head_major):
    t = x.shape[0]
    mod_rows = scale.shape[0]
    if mod_rows == 1:
        mod_spec = pl.BlockSpec((1, D_MODEL), lambda i: (0, 0))
    else:
        mod_spec = pl.BlockSpec((tb, D_MODEL), lambda i: (i, 0))
    const = lambda shape: pl.BlockSpec(shape, lambda i: (0, 0))
    tok = lambda w: (pl.BlockSpec((tb, w), lambda i: (i, 0)), (t, w))
    if head_major:
        feat_t = lambda rows: (pl.BlockSpec((A_HEADS, rows, tb), lambda i: (0, 0, i)), (A_HEADS, rows, t))
        tok_h = (pl.BlockSpec((A_HEADS, tb, A_VDIM), lambda i: (0, i, 0)), (A_HEADS, t, A_VDIM))
        q_out, k_out, v_out = feat_t(A_VDIM), tok_h, feat_t(VT_ROWS)
    else:
        q_out = k_out = v_out = tok(D_MODEL)
    outs = [
        (tok(3 * D_MODEL), BF16),
        (tok(D_MODEL), F32),
        (tok(2 * D_MODEL), F32),
        (tok(LANES), F32),
        (q_out, BF16),
        (q_out, BF16),
        (tok(D_MODEL), F32),
        (k_out, BF16),
        (tok(D_MODEL), F32),
        (v_out, BF16),
    ]
    return pl.pallas_call(
        functools.partial(_inproj_kernel, tb=tb, pos0=pos0, period=period, head_major=head_major),
        grid=(t // tb,),
        in_specs=[tok(D_MODEL)[0], mod_spec, mod_spec, const((1, D_MODEL)),
                  const((D_MODEL, N_MAIN_COLS)), const((D_MODEL, LANES)), const((1, LANES)), const((1, LANES))],
        out_specs=[spec for (spec, _), _ in outs],
        out_shape=[jax.ShapeDtypeStruct(shape, dt) for (_, shape), dt in outs],
        compiler_params=_cparams(("parallel",)),
        name="in_proj",
    )(x, scale, shift, g_pre, w_main, w_if, b_if128, inv128)


def _mlstm_kernel(qkv_ref, gates_ref, mo_ref, nw_ref, c0_ref, n0_ref, m0_ref,
                  ya_ref, c_ref, n_ref, m_ref, *, L):
    @pl.when(pl.program_id(1) == 0)
    def _():
        c_ref[...] = c0_ref[...]
        n_ref[...] = n0_ref[...]
        m_ref[...] = m0_ref[...]

    gates = gates_ref[...]
    r = lax.broadcasted_iota(jnp.int32, (L, L), 0)
    c = lax.broadcasted_iota(jnp.int32, (L, L), 1)
    causal = c <= r
    tril = jnp.where(causal, 1.0, 0.0).astype(BF16)
    eye8 = jnp.where(lax.broadcasted_iota(jnp.int32, (8, LANES), 0)
                     == lax.broadcasted_iota(jnp.int32, (8, LANES), 1), 1.0, 0.0).astype(BF16)
    g3 = _split3(gates)
    cums = sum(jnp.dot(tril, gi, preferred_element_type=F32) for gi in g3)
    gt8 = sum(_nt(eye8, gi) for gi in g3)
    bt8 = sum(_nt(eye8, ci) for ci in _split3(cums))

    for h in range(M_HEADS):
        sl = slice(h * M_HEAD_DIM, (h + 1) * M_HEAD_DIM)
        q = qkv_ref[:, h * M_HEAD_DIM:(h + 1) * M_HEAD_DIM]
        k = qkv_ref[:, D_MODEL + h * M_HEAD_DIM:D_MODEL + (h + 1) * M_HEAD_DIM]
        v = qkv_ref[:, 2 * D_MODEL + h * M_HEAD_DIM:2 * D_MODEL + (h + 1) * M_HEAD_DIM]
        ig_col = gates[:, h:h + 1]
        ig_row = gt8[h:h + 1, :]
        b_col = cums[:, M_HEADS + h:M_HEADS + h + 1]
        b_row = bt8[M_HEADS + h:M_HEADS + h + 1, :]
        m_prev = m_ref[0, h:h + 1, 0:1]
        c_old = c_ref[0, h]
        n_old = n_ref[0, h:h + 1, :]

        dmat = jnp.where(causal, b_col - b_row + ig_row, NEG)
        m_inter = b_col + m_prev
        m_t = jnp.maximum(m_inter, jnp.max(dmat, axis=-1, keepdims=True))
        w_inter = jnp.exp(m_inter - m_t)
        a = jnp.exp(dmat - m_t) * _nt(q, k)
        num = w_inter * _nt(q, c_old.astype(BF16)) + jnp.dot(a.astype(BF16), v, preferred_element_type=F32)
        qn = jnp.sum(q.astype(F32) * n_old, axis=-1, keepdims=True)
        den = w_inter * qn + jnp.sum(a, axis=-1, keepdims=True)
        hm = num / jnp.maximum(jnp.abs(den), jnp.exp(-m_t))

        b_last = b_col[L - 1:L, :]
        g_row = b_last - b_row + ig_row
        g_col = b_last - b_col + ig_col
        m_new = jnp.maximum(b_last + m_prev, jnp.max(g_row, axis=-1, keepdims=True))
        dec = jnp.exp(b_last + m_prev - m_new)
        w_col = jnp.exp(g_col - m_new)
        vw = (v.astype(F32) * w_col).astype(BF16)
        c_ref[0, h] = dec * c_old + _tn(vw, k)
        n_ref[0, h:h + 1, :] = dec * n_old + jnp.sum(k.astype(F32) * w_col, axis=0, keepdims=True)
        m_ref[0, h:h + 1, :] = jnp.broadcast_to(m_new, (1, LANES))

        ya_ref[:, sl] = _rms(hm, nw_ref[:, sl]) * jax.nn.sigmoid(mo_ref[:, sl])


def _mlstm(qkv, gates, mo, norm_w, c0, n0, m0, *, batch, L):
    t_total = qkv.shape[0]
    nc = t_total // (batch * L)
    tok = lambda w: pl.BlockSpec((L, w), lambda b, c: (b * nc + c, 0))
    st4 = pl.BlockSpec((1, M_HEADS, M_HEAD_DIM, M_HEAD_DIM), lambda b, c: (b, 0, 0, 0))
    st3n = pl.BlockSpec((1, M_HEADS, M_HEAD_DIM), lambda b, c: (b, 0, 0))
    st3m = pl.BlockSpec((1, M_HEADS, LANES), lambda b, c: (b, 0, 0))
    return pl.pallas_call(
        functools.partial(_mlstm_kernel, L=L),
        grid=(batch, nc),
        in_specs=[tok(3 * D_MODEL), tok(LANES), tok(D_MODEL), pl.BlockSpec((1, D_MODEL), lambda b, c: (0, 0)),
                  st4, st3n, st3m],
        out_specs=[tok(D_MODEL), st4, st3n, st3m],
        out_shape=[jax.ShapeDtypeStruct((t_total, D_MODEL), F32),
                   jax.ShapeDtypeStruct(c0.shape, F32),
                   jax.ShapeDtypeStruct(n0.shape, F32),
                   jax.ShapeDtypeStruct(m0.shape, F32)],
        compiler_params=_cparams(("parallel", "arbitrary")),
        name="mlstm",
    )(qkv, gates, mo, norm_w, c0, n0, m0)


def _lambda_full(lq1_ref, lk1_ref, lq2_ref, lk2_ref):
    a = jnp.exp(jnp.sum(lq1_ref[...] * lk1_ref[...], axis=-1, keepdims=True))
    b = jnp.exp(jnp.sum(lq2_ref[...] * lk2_ref[...], axis=-1, keepdims=True))
    return a - b + LAMBDA_INIT


def _attn_prompt_kernel(qi_ref, kj_ref, q1_ref, q2_ref, k_ref, vt_ref, lq1_ref, lk1_ref, lq2_ref, lk2_ref, dn_ref,
                        o_ref, m_sc, a_sc, acc_sc, s_sc, p_sc, *, tq):
    step = pl.program_id(0)
    qi = qi_ref[step]
    kj = kj_ref[step]
    n_sub = k_ref.shape[1] // tq
    last_kj = qi // n_sub
    q_refs = (q1_ref, q2_ref)

    @pl.when(kj == 0)
    def _():
        m_sc[...] = jnp.full(m_sc.shape, NEG, F32)
        acc_sc[...] = jnp.zeros(acc_sc.shape, F32)

    def run(subs):
        n = 2 * len(subs)

        def scores(head, g):
            sub = subs[g // 2][0]
            s_sc[g] = jnp.dot(k_ref[head, sub * tq:(sub + 1) * tq, :], q_refs[g % 2][head],
                              preferred_element_type=F32)

        def softmax(head, g):
            pair = 2 * head + g % 2
            s = s_sc[g]
            if subs[g // 2][1]:
                key = lax.broadcasted_iota(jnp.int32, (tq, tq), 0)
                qry = lax.broadcasted_iota(jnp.int32, (tq, tq), 1)
                s = jnp.where(key <= qry, s, NEG)
            m_prev = m_sc[pair]
            m_new = jnp.maximum(m_prev, jnp.max(s, axis=0, keepdims=True))
            a_sc[g] = jnp.exp2(m_prev - m_new)
            p_sc[g] = jnp.exp2(s - m_new).astype(BF16)
            m_sc[pair] = m_new

        def values(head, g):
            pair = 2 * head + g % 2
            sub = subs[g // 2][0]
            pv = jnp.dot(vt_ref[head, :, sub * tq:(sub + 1) * tq], p_sc[g], preferred_element_type=F32)
            acc_sc[pair] = a_sc[g] * acc_sc[pair] + pv

        for g in range(n):
            scores(0, g)
        softmax(0, 0)

        def body(head, carry):
            for g in range(n):
                values(head, g)
                if g + 1 < n:
                    softmax(head, g + 1)
                else:
                    softmax(head + 1, 0)
                scores(head + 1, g)
            return carry

        lax.fori_loop(0, A_HEADS - 1, body, 0)
        for g in range(n):
            values(A_HEADS - 1, g)
            if g + 1 < n:
                softmax(A_HEADS - 1, g + 1)

    @pl.when(kj < last_kj)
    def _():
        run(tuple((sub, False) for sub in range(n_sub)))

    for diag in range(n_sub):
        @pl.when((kj == last_kj) & (qi % n_sub == diag))
        def _(diag=diag):
            run(tuple((sub, False) for sub in range(diag)) + ((diag, True),))

    @pl.when(kj == last_kj)
    def _():
        lam = _lambda_full(lq1_ref, lk1_ref, lq2_ref, lk2_ref)
        for h in range(A_HEADS):
            a1 = acc_sc[2 * h]
            a2 = acc_sc[2 * h + 1]
            ot = (a1[0:A_VDIM] / a1[A_VDIM:A_VDIM + 1]
                  - lam * (a2[0:A_VDIM] / a2[A_VDIM:A_VDIM + 1]))
            o_ref[:, h * LANES:(h + 1) * LANES] = _rms(ot.T, dn_ref[...]) * (1.0 - LAMBDA_INIT)


def _attn_prompt(q1t, q2t, kh, vt, lams, diff_norm, *, tq, n_sub):
    t = kh.shape[1]
    nq = t // tq
    tk = n_sub * tq
    assert t % tk == 0
    steps = [(qi, kj) for qi in range(nq) for kj in range(qi // n_sub + 1)]
    qi_of = jnp.asarray([s[0] for s in steps], jnp.int32)
    kj_of = jnp.asarray([s[1] for s in steps], jnp.int32)
    qspec = pl.BlockSpec((A_HEADS, A_VDIM, tq), lambda s, qi, kj: (0, 0, qi[s]))
    kspec = pl.BlockSpec((A_HEADS, tk, A_VDIM), lambda s, qi, kj: (0, kj[s], 0))
    vspec = pl.BlockSpec((A_HEADS, VT_ROWS, tk), lambda s, qi, kj: (0, 0, kj[s]))
    small = lambda n: pl.BlockSpec((1, n), lambda s, qi, kj: (0, 0))
    grid_spec = pltpu.PrefetchScalarGridSpec(
        num_scalar_prefetch=2,
        grid=(len(steps),),
        in_specs=[qspec, qspec, kspec, vspec] + [small(A_HEAD_DIM)] * 4 + [small(A_VDIM)],
        out_specs=pl.BlockSpec((tq, D_MODEL), lambda s, qi, kj: (qi[s], 0)),
        scratch_shapes=[pltpu.VMEM((2 * A_HEADS, 1, tq), F32),
                        pltpu.VMEM((2 * n_sub, 1, tq), F32),
                        pltpu.VMEM((2 * A_HEADS, VT_ROWS, tq), F32),
                        pltpu.VMEM((2 * n_sub, tq, tq), F32), pltpu.VMEM((2 * n_sub, tq, tq), BF16)],
    )
    return pl.pallas_call(
        functools.partial(_attn_prompt_kernel, tq=tq),
        grid_spec=grid_spec,
        out_shape=jax.ShapeDtypeStruct((t, D_MODEL), F32),
        compiler_params=_cparams(("arbitrary",)),
        name="diff_attn_prompt",
    )(qi_of, kj_of, q1t, q2t, kh, vt, *lams, diff_norm)


def _attn_sample_kernel(pt_ref, q_ref, kn_ref, vn_ref, lq1_ref, lk1_ref, lq2_ref, lk2_ref, dn_ref, *rest,
                        ppb, n_new, page):
    k_refs = rest[:ppb]
    v_refs = rest[ppb:2 * ppb]
    o_ref, m_sc, l_sc, acc_sc = rest[2 * ppb:]
    j = pl.program_id(1)
    rows = q_ref.shape[2]

    @pl.when(j == 0)
    def _():
        m_sc[...] = jnp.full(m_sc.shape, NEG, F32)
        l_sc[...] = jnp.zeros(l_sc.shape, F32)
        acc_sc[...] = jnp.zeros(acc_sc.shape, F32)

    def update(scores, values):
        s = jnp.concatenate([scores(h) for h in range(A_HEADS)], axis=0)
        m_prev = m_sc[...]
        m_new = jnp.maximum(m_prev, jnp.max(s, axis=-1, keepdims=True))
        alpha = jnp.exp(m_prev - m_new)
        p = jnp.exp(s - m_new)
        l_sc[...] = alpha * l_sc[...] + jnp.sum(p, axis=-1, keepdims=True)
        pb = p.astype(BF16)
        pv = jnp.concatenate([jnp.dot(pb[h * rows:(h + 1) * rows], values(h), preferred_element_type=F32)
                              for h in range(A_HEADS)], axis=0)
        acc_sc[...] = alpha * acc_sc[...] + pv
        m_sc[...] = m_new

    def head_rows(refs, h):
        parts = [r[0, pl.ds(h, page, stride=A_HEADS), :] for r in refs]
        return jnp.concatenate(parts, axis=0).astype(BF16)

    update(lambda h: _nt(q_ref[0, h], head_rows(k_refs, h)), lambda h: head_rows(v_refs, h))

    @pl.when(j == pl.num_programs(1) - 1)
    def _():
        lam = _lambda_full(lq1_ref, lk1_ref, lq2_ref, lk2_ref)
        r = lax.broadcasted_iota(jnp.int32, (rows, rows), 0)
        c = lax.broadcasted_iota(jnp.int32, (rows, rows), 1)
        keep = (c <= r % n_new) & (c < n_new)
        update(lambda h: jnp.where(keep, _nt(q_ref[0, h], kn_ref[0, h]), NEG), lambda h: vn_ref[0, h])
        o = acc_sc[...] / l_sc[...]
        for h in range(A_HEADS):
            od = o[h * rows:h * rows + n_new, :] - lam * o[h * rows + n_new:h * rows + 2 * n_new, :]
            o_ref[0, :, h * LANES:(h + 1) * LANES] = _rms(od, dn_ref[...]) * (1.0 - LAMBDA_INIT)


def _attn_sample(q_rows, k_new, v_new, cache_k, cache_v, page_table, lams, diff_norm, *, ppb, n_new, page_base):
    bsz, n_pages = page_table.shape
    page = cache_k.shape[1] // A_HEADS
    rows = q_rows.shape[2]
    seq = lambda: pl.BlockSpec((1, A_HEADS, rows, A_VDIM), lambda b, j, pt: (b, 0, 0, 0))
    small = lambda n: pl.BlockSpec((1, n), lambda b, j, pt: (0, 0))

    def page_spec(i):
        return pl.BlockSpec((1, page * A_HEADS, A_VDIM), lambda b, j, pt: (page_base + pt[b, j * ppb + i], 0, 0))

    grid_spec = pltpu.PrefetchScalarGridSpec(
        num_scalar_prefetch=1,
        grid=(bsz, n_pages // ppb),
        in_specs=[seq(), seq(), seq()] + [small(A_HEAD_DIM)] * 4 + [small(A_VDIM)]
                 + [page_spec(i) for i in range(ppb)] + [page_spec(i) for i in range(ppb)],
        out_specs=pl.BlockSpec((1, n_new, D_MODEL), lambda b, j, pt: (b, 0, 0)),
        scratch_shapes=[pltpu.VMEM((A_HEADS * rows, 1), F32),
                        pltpu.VMEM((A_HEADS * rows, 1), F32),
                        pltpu.VMEM((A_HEADS * rows, A_VDIM), F32)],
    )
    return pl.pallas_call(
        functools.partial(_attn_sample_kernel, ppb=ppb, n_new=n_new, page=page),
        grid_spec=grid_spec,
        out_shape=jax.ShapeDtypeStruct((bsz, n_new, D_MODEL), F32),
        compiler_params=_cparams(("parallel", "arbitrary")),
        name="diff_attn_sample",
    )(page_table, q_rows, k_new, v_new, *lams, diff_norm, *([cache_k] * ppb), *([cache_v] * ppb))


def _postmix_kernel(x_ref, ya_ref, yb_ref, gab_ref, g1_ref, sc2_ref, sh2_ref, gpost_ref, gpre_ref, wo_ref, wq_ref,
                    x1_ref, h2_ref, qp_ref):
    merged = (jax.nn.sigmoid(gab_ref[:, 0:D_MODEL]) * ya_ref[...]
              + jax.nn.sigmoid(gab_ref[:, D_MODEL:2 * D_MODEL]) * yb_ref[...])
    o = jnp.dot(merged.astype(BF16), wo_ref[...], preferred_element_type=F32)
    x1 = x_ref[...] + g1_ref[...] * _rms(o, gpost_ref[...])
    x1_ref[...] = x1
    h2 = (_rms(x1, gpre_ref[...]) * (1.0 + sc2_ref[...]) + sh2_ref[...]).astype(BF16)
    h2_ref[...] = h2
    qp_ref[...] = jnp.dot(h2, wq_ref[...], preferred_element_type=F32).astype(BF16)


def _postmix(x, ya, yb, gab, gate1, scale2, shift2, g_post1, g_pre2, w_out, w_q, *, tb):
    t = x.shape[0]
    nq = w_q.shape[1]
    if gate1.shape[0] == 1:
        mod_spec = pl.BlockSpec((1, D_MODEL), lambda i: (0, 0))
    else:
        mod_spec = pl.BlockSpec((tb, D_MODEL), lambda i: (i, 0))
    const = lambda shape: pl.BlockSpec(shape, lambda i: (0, 0))
    tok = lambda w: pl.BlockSpec((tb, w), lambda i: (i, 0))
    return pl.pallas_call(
        _postmix_kernel,
        grid=(t // tb,),
        in_specs=[tok(D_MODEL), tok(D_MODEL), tok(D_MODEL), tok(2 * D_MODEL), mod_spec, mod_spec, mod_spec,
                  const((1, D_MODEL)), const((1, D_MODEL)), const((D_MODEL, D_MODEL)), const((D_MODEL, nq))],
        out_specs=[tok(D_MODEL), tok(D_MODEL), tok(nq)],
        out_shape=[jax.ShapeDtypeStruct((t, D_MODEL), F32),
                   jax.ShapeDtypeStruct((t, D_MODEL), BF16),
                   jax.ShapeDtypeStruct((t, nq), BF16)],
        compiler_params=_cparams(("parallel",)),
        name="post_mix",
    )(x, ya, yb, gab, gate1, scale2, shift2, g_post1, g_pre2, w_out, w_q)


NOT_TOP = 255.0


def _top_ranked(s, n):
    vals = []
    work = s
    rank = jnp.full(s.shape, NOT_TOP, F32)
    for i in range(n):
        mx = jnp.max(work, axis=0, keepdims=True)
        hit = work == mx
        rank = jnp.where(hit, float(i), rank)
        work = jnp.where(hit, NEG, work)
        vals.append(mx)
    return vals, rank


def _peer_score_kernel(qp_ref, k1_ref, k2_ref, c1_ref, e1_ref, r2_ref, e2_ref, *, tt):
    ab = N_KEYS // 8
    jrow = lax.broadcasted_iota(jnp.int32, (8, tt), 0)
    for h in range(PEER_HEADS):
        base = h * 2 * N_KEYS
        s1 = _nt(k1_ref[h], qp_ref[:, base:base + N_KEYS])
        s2 = _nt(k2_ref[h], qp_ref[:, base + N_KEYS:base + 2 * N_KEYS])
        t1, _ = _top_ranked(s1, PEER_TOPK)
        t2, rank2 = _top_ranked(s2, PEER_TOPK)
        t2a = jnp.concatenate(t2[:8], axis=0)
        t2b = jnp.concatenate(t2[8:], axis=0)
        slabs = [t1[0] + t2a, t1[0] + t2b]
        for i in range(1, PEER_TOPK):
            slabs.append(jnp.where(jrow < PEER_TOPK // (i + 1), t1[i] + t2a, NEG))
        cand = jnp.concatenate(slabs, axis=0)
        tau = _top_ranked(cand, PEER_TOPK)[0][-1]
        top = t1[0] + t2[0]
        sel = cand >= tau
        z = jnp.sum(jnp.where(sel, jnp.exp(cand - top), 0.0), axis=0, keepdims=True)
        ones = jnp.where(sel, 1.0, 0.0)
        counts = [jnp.sum(ones[0:16], axis=0, keepdims=True)]
        for i in range(1, PEER_TOPK):
            counts.append(jnp.sum(ones[8 + 8 * i:16 + 8 * i], axis=0, keepdims=True))
        c1 = jnp.zeros(s1.shape, F32)
        for i in range(PEER_TOPK):
            c1 = jnp.where(s1 == t1[i], counts[i], c1)
        c1_ref[:, h] = c1.reshape(ab, 8, tt)
        e1_ref[:, h] = (jnp.exp(s1 - t1[0]) * (0.5 / z)).reshape(ab, 8, tt)
        r2_ref[h] = rank2.astype(BF16)
        e2_ref[h] = jnp.exp(s2 - t2[0]).astype(BF16)


def _peer_scores(qp, k1, k2, *, tt):
    t = qp.shape[0]
    ab = N_KEYS // 8
    a_spec = pl.BlockSpec((ab, PEER_HEADS, 8, tt), lambda i: (0, 0, 0, i))
    b_spec = pl.BlockSpec((PEER_HEADS, N_KEYS, tt), lambda i: (0, 0, i))
    kspec = pl.BlockSpec((PEER_HEADS, N_KEYS, N_KEYS), lambda i: (0, 0, 0))
    a_shape = jax.ShapeDtypeStruct((ab, PEER_HEADS, 8, t), F32)
    b_shape = jax.ShapeDtypeStruct((PEER_HEADS, N_KEYS, t), BF16)
    return pl.pallas_call(
        functools.partial(_peer_score_kernel, tt=tt),
        grid=(t // tt,),
        in_specs=[pl.BlockSpec((tt, qp.shape[1]), lambda i: (i, 0)), kspec, kspec],
        out_specs=[a_spec, a_spec, b_spec, b_spec],
        out_shape=[a_shape, a_shape, b_shape, b_shape],
        compiler_params=_cparams(("parallel",)),
        name="peer_scores",
    )(qp, k1, k2)


def _gelu_x2(x):
    return x * (1.0 + lax.erf(x * (2.0 ** -0.5)))


def _peer_dense_kernel(h2_ref, u_ref, v_ref, c1_ref, e1_ref, r2_ref, e2_ref, x1_ref, g2_ref, gpost_ref,
                       y_ref, acc_sc, a_sc, *, a_per):
    j = pl.program_id(1)

    @pl.when(j == 0)
    def _():
        acc_sc[...] = jnp.zeros(acc_sc.shape, F32)

    zt = _nt(u_ref[...], h2_ref[...])
    zero = jnp.zeros((), BF16)
    tt = h2_ref.shape[0]

    def key_rows(ref, al, h):
        row = ref[al // 8, h, al % 8:al % 8 + 1, :]
        tile = jnp.broadcast_to(row, (16, tt)).astype(BF16)
        return jnp.tile(tile, (N_KEYS // 16, 1))

    for al in range(a_per):
        w = None
        for h in range(PEER_HEADS):
            term = jnp.where(r2_ref[h] < key_rows(c1_ref, al, h), e2_ref[h] * key_rows(e1_ref, al, h), zero)
            w = term if w is None else w + term
        rows = slice(al * N_KEYS, (al + 1) * N_KEYS)
        a_sc[rows, :] = w * _gelu_x2(zt[rows, :]).astype(BF16)
    acc_sc[...] += _tn(v_ref[...], a_sc[...])

    @pl.when(j == pl.num_programs(1) - 1)
    def _():
        f = acc_sc[...].T
        y_ref[...] = x1_ref[...] + g2_ref[...] * _rms(f, gpost_ref[...])


def _peer_dense(h2, u_b, v_b, c1, e1, r2, e2, x1, gate2, g_post2, *, tt, a_per):
    t = h2.shape[0]
    n_exp = u_b.shape[0]
    eb = a_per * N_KEYS
    if gate2.shape[0] == 1:
        mod_spec = pl.BlockSpec((1, D_MODEL), lambda i, j: (0, 0))
    else:
        mod_spec = pl.BlockSpec((tt, D_MODEL), lambda i, j: (i, 0))
    a_spec = pl.BlockSpec((a_per // 8, PEER_HEADS, 8, tt), lambda i, j: (j, 0, 0, i))
    b_spec = pl.BlockSpec((PEER_HEADS, N_KEYS, tt), lambda i, j: (0, 0, i))
    return pl.pallas_call(
        functools.partial(_peer_dense_kernel, a_per=a_per),
        grid=(t // tt, n_exp // eb),
        in_specs=[pl.BlockSpec((tt, D_MODEL), lambda i, j: (i, 0)),
                  pl.BlockSpec((eb, D_MODEL), lambda i, j: (j, 0)),
                  pl.BlockSpec((eb, D_MODEL), lambda i, j: (j, 0)),
                  a_spec, a_spec, b_spec, b_spec,
                  pl.BlockSpec((tt, D_MODEL), lambda i, j: (i, 0)),
                  mod_spec,
                  pl.BlockSpec((1, D_MODEL), lambda i, j: (0, 0))],
        out_specs=pl.BlockSpec((tt, D_MODEL), lambda i, j: (i, 0)),
        out_shape=jax.ShapeDtypeStruct((t, D_MODEL), F32),
        scratch_shapes=[pltpu.VMEM((D_MODEL, tt), F32), pltpu.VMEM((eb, tt), BF16)],
        compiler_params=_cparams(("parallel", "arbitrary")),
        name="peer_dense",
    )(h2, u_b, v_b, c1, e1, r2, e2, x1, gate2, g_post2)


def _block(t, pref):
    return pref if t % pref == 0 else t


def _layer(x, mods, weights, *, pos0, period, run_mlstm, attend, head_major):
    shift1, scale1, gate1, shift2, scale2, gate2 = mods
    w = weights
    t = x.shape[0]
    tb = _block(t, 256)
    qkv, mo, gab, gates, q1, q2, kf, kb, vf, vb = _inproj(
        x, scale1, shift1, w["g_pre1"], w["w_main"], w["w_if"], w["b_if128"], w["inv128"],
        pos0=pos0, period=period, tb=tb, head_major=head_major)

    ya, c_new, n_new, m_new = run_mlstm(qkv, gates, mo)
    yb = attend(q1, q2, kb, vb)

    x1, h2, qp = _postmix(x, ya, yb, gab, gate1, scale2, shift2, w["g_post1"], w["g_pre2"], w["w_out"], w["w_q"], tb=tb)
    tt = _block(t, 128)
    c1, e1, r2, e2 = _peer_scores(qp, w["k1"], w["k2"], tt=tt)
    ttd = _block(t, 512)
    y = _peer_dense(h2, w["u"], w["v"], c1, e1, r2, e2, x1, gate2, w["g_post2"], tt=ttd, a_per=8)
    return y, kf, vf, c_new, n_new, m_new


def _hybrid_step(x_prompt, x_sample, cache_k, cache_v, state_C, state_n, state_m, page_table, c_prompt, c_sample,
                 w_ada, b_ada, g_pre1, g_post1, g_pre2, g_post2, w_in, b_if, mlstm_norm,
                 lambda_q1, lambda_k1, lambda_q2, lambda_k2, diff_norm, w_out,
                 peer_wq, peer_k1, peer_k2, peer_u, peer_v, *, m_chunk=128, tq=512, attn_sub=4, ppb=16):
    l = 0
    bp, sp, _ = x_prompt.shape
    bs, ts, _ = x_sample.shape
    assert bp == 1
    row = lambda a: a.reshape(1, -1)

    wi = w_in[l]
    n_if = 2 * M_HEADS
    w_main = jnp.concatenate([wi[:, :4 * D_MODEL], wi[:, 4 * D_MODEL + n_if:]], axis=1).astype(BF16)
    w_if = jnp.pad(wi[:, 4 * D_MODEL:4 * D_MODEL + n_if], ((0, 0), (0, LANES - n_if))).astype(BF16)
    b_if128 = jnp.pad(b_if[l], (0, LANES - n_if)).reshape(1, LANES)
    inv = ROPE_THETA ** (-jnp.arange(0, A_HEAD_DIM, 2, dtype=F32) / A_HEAD_DIM)
    inv128 = jnp.tile(inv, LANES // inv.shape[0]).reshape(1, LANES)
    weights = dict(
        g_pre1=row(g_pre1[l]), g_post1=row(g_post1[l]), g_pre2=row(g_pre2[l]), g_post2=row(g_post2[l]),
        w_main=w_main, w_if=w_if, b_if128=b_if128, inv128=inv128,
        w_out=w_out[l].astype(BF16), w_q=peer_wq[l].astype(BF16),
        k1=peer_k1[l].astype(BF16), k2=peer_k2[l].astype(BF16),
        u=peer_u[l].astype(BF16), v=peer_v[l].astype(BF16))
    lams = [row(a[l]) for a in (lambda_q1, lambda_k1, lambda_q2, lambda_k2)]
    dn = row(diff_norm[l])
    nw = row(mlstm_norm[l])

    n_c = bp + bs
    c_all = jnp.pad(jnp.concatenate([c_prompt, c_sample], axis=0), ((0, (-n_c) % 8), (0, 0)))
    mod = _ada(c_all, w_ada[l], b_ada[l])
    mods_p = [mod[0:1, i * D_MODEL:(i + 1) * D_MODEL] for i in range(6)]
    mods_s = [jnp.repeat(mod[bp:n_c, i * D_MODEL:(i + 1) * D_MODEL], ts, axis=0) for i in range(6)]

    def run_mlstm_p(qkv, gates, mo):
        c0 = jnp.zeros((bp, M_HEADS, M_HEAD_DIM, M_HEAD_DIM), F32)
        n0 = jnp.zeros((bp, M_HEADS, M_HEAD_DIM), F32)
        m0 = jnp.zeros((bp, M_HEADS, LANES), F32)
        return _mlstm(qkv, gates, mo, nw, c0, n0, m0, batch=bp, L=math.gcd(sp, m_chunk))

    def attend_p(q1t, q2t, kh, vt):
        tq_p = _block(sp, tq)
        return _attn_prompt(q1t, q2t, kh, vt, lams, dn, tq=tq_p, n_sub=math.gcd(sp // tq_p, attn_sub))

    yp, kp, vp, cp, n_p, mp = _layer(
        x_prompt.reshape(sp, D_MODEL), mods_p, weights, pos0=0, period=sp,
        run_mlstm=run_mlstm_p, attend=attend_p, head_major=True)

    lpad = 16
    past_len = page_table.shape[1] * cache_k.shape[2]

    def pad_tokens(a, fill_row=None):
        a3 = a.reshape(bs, ts, a.shape[-1])
        if fill_row is None:
            out = jnp.pad(a3, ((0, 0), (0, lpad - ts), (0, 0)))
        else:
            fill = jnp.broadcast_to(fill_row.astype(a.dtype), (bs, lpad - ts, a.shape[-1]))
            out = jnp.concatenate([a3, fill], axis=1)
        return out.reshape(bs * lpad, a.shape[-1])

    def run_mlstm_s(qkv, gates, mo):
        lane = jnp.arange(LANES)
        gate_fill = jnp.where(lane < M_HEADS, NEG, 0.0).astype(F32)
        m0 = jnp.broadcast_to(state_m[l][:, :, None], (bs, M_HEADS, LANES))
        ya, c_new, n_new, m_new = _mlstm(pad_tokens(qkv), pad_tokens(gates, gate_fill), pad_tokens(mo), nw,
                                         state_C[l], state_n[l], m0, batch=bs, L=lpad)
        ya = ya.reshape(bs, lpad, D_MODEL)[:, :ts].reshape(bs * ts, D_MODEL)
        return ya, c_new, n_new, m_new

    def attend_s(q1, q2, kb, vb):
        heads = lambda a: a.reshape(bs, ts, A_HEADS, A_VDIM).transpose(0, 2, 1, 3)
        zeros = lambda n: jnp.zeros((bs, A_HEADS, n, A_VDIM), BF16)
        q_rows = jnp.concatenate([heads(q1), heads(q2), zeros(lpad - 2 * ts)], axis=2)
        k_new = jnp.concatenate([heads(kb), zeros(lpad - ts)], axis=2)
        v_new = jnp.concatenate([heads(vb), zeros(lpad - ts)], axis=2)
        depth, n_pool, page = cache_k.shape[0], cache_k.shape[1], cache_k.shape[2]
        ck = cache_k.reshape(depth * n_pool, page * A_HEADS, A_VDIM)
        cv = cache_v.reshape(depth * n_pool, page * A_HEADS, A_VDIM)
        o = _attn_sample(q_rows, k_new, v_new, ck, cv, page_table, lams, dn,
                         ppb=math.gcd(page_table.shape[1], ppb), n_new=ts, page_base=l * n_pool)
        return o.reshape(bs * ts, D_MODEL)

    ys, k_s, v_s, cs, n_s, ms = _layer(
        x_sample.reshape(bs * ts, D_MODEL), mods_s, weights, pos0=past_len, period=ts,
        run_mlstm=run_mlstm_s, attend=attend_s, head_major=False)

    return (yp.reshape(bp, sp, D_MODEL), ys.reshape(bs, ts, D_MODEL),
            kp.reshape(1, bp, sp, A_HEADS, A_VDIM), vp.reshape(1, bp, sp, A_HEADS, A_VDIM),
            cp[None], n_p[None], mp[None, :, :, 0],
            k_s.reshape(1, bs, ts, A_HEADS, A_VDIM), v_s.reshape(1, bs, ts, A_HEADS, A_VDIM),
            cs[None], n_s[None], ms[None, :, :, 0])


def kernel(x_prompt, x_sample, cache_k, cache_v, state_C, state_n, state_m, page_table, c_prompt, c_sample, w_ada, b_ada, g_pre1, g_post1, g_pre2, g_post2, w_in, b_if, mlstm_norm, lambda_q1, lambda_k1, lambda_q2, lambda_k2, diff_norm, w_out, peer_wq, peer_k1, peer_k2, peer_u, peer_v):
    return _hybrid_step(x_prompt, x_sample, cache_k, cache_v, state_C, state_n, state_m, page_table, c_prompt, c_sample,
                        w_ada, b_ada, g_pre1, g_post1, g_pre2, g_post2, w_in, b_if, mlstm_norm,
                        lambda_q1, lambda_k1, lambda_q2, lambda_k2, diff_norm, w_out,
                        peer_wq, peer_k1, peer_k2, peer_u, peer_v)
```

```python
import functools
import math

import jax
import jax.numpy as jnp
from jax import lax
from jax.experimental import pallas as pl
from jax.experimental.pallas import tpu as pltpu

F32 = jnp.float32
BF16 = jnp.bfloat16

D_MODEL = 1024
M_HEADS = 4
M_HEAD_DIM = 256
A_HEADS = 8
A_HEAD_DIM = 64
A_VDIM = 128
ROPE_THETA = 10000.0
N_KEYS = 128
PEER_HEADS = 8
PEER_TOPK = 16
NORM_EPS = 1e-6
LAMBDA_INIT = 0.8 - 0.6 * math.exp(-0.3 * 0)
N_MAIN_COLS = 9 * D_MODEL

VT_ROWS = A_VDIM + 16
LANES = 128
NEG = -1e30
VMEM_LIMIT = 56 * 1024 * 1024


def _cparams(sem, flags=None):
    return pltpu.CompilerParams(dimension_semantics=sem, vmem_limit_bytes=VMEM_LIMIT, flags=flags)


def _nt(a, b):
    return lax.dot_general(a, b, (((1,), (1,)), ((), ())), preferred_element_type=F32)


def _tn(a, b):
    return lax.dot_general(a, b, (((0,), (0,)), ((), ())), preferred_element_type=F32)


def _rms(x, g):
    return x * lax.rsqrt(jnp.mean(x * x, axis=-1, keepdims=True) + NORM_EPS) * g


def _split3(x):
    x1 = x.astype(BF16)
    r1 = x - x1.astype(F32)
    x2 = r1.astype(BF16)
    x3 = (r1 - x2.astype(F32)).astype(BF16)
    return x1, x2, x3


def _ada_kernel(c_ref, w_ref, b_ref, o_ref):
    c = c_ref[...]
    s = c * jax.nn.sigmoid(c)
    o_ref[...] = jnp.dot(s.astype(BF16), w_ref[...].astype(BF16), preferred_element_type=F32) + b_ref[...]


def _ada(c_all, w_ada, b_ada):
    r = c_all.shape[0]
    n = w_ada.shape[1]
    bn = D_MODEL
    return pl.pallas_call(
        _ada_kernel,
        grid=(n // bn,),
        in_specs=[pl.BlockSpec((r, D_MODEL), lambda j: (0, 0)),
                  pl.BlockSpec((D_MODEL, bn), lambda j: (0, j)),
                  pl.BlockSpec((1, bn), lambda j: (0, j))],
        out_specs=pl.BlockSpec((r, bn), lambda j: (0, j)),
        out_shape=jax.ShapeDtypeStruct((r, n), F32),
        compiler_params=_cparams(("arbitrary",)),
        name="ada_mod",
    )(c_all, w_ada, b_ada.reshape(1, n))


def _rope128(x, cos, sin_signed):
    fwd = pltpu.roll(x, LANES - 32, axis=1)
    bwd = pltpu.roll(x, 32, axis=1)
    lane = lax.broadcasted_iota(jnp.int32, x.shape, 1)
    rot = jnp.where((lane % 64) < 32, fwd, bwd)
    return x * cos + rot * sin_signed


def _inproj_kernel(x_ref, sc_ref, sh_ref, g_ref, wm_ref, wif_ref, bif_ref, inv_ref,
                   qkv_ref, mo_ref, gab_ref, gates_ref, q1_ref, q2_ref, kf_ref, kb_ref, vf_ref, vb_ref,
                   *, tb, pos0, period, head_major):
    x = x_ref[...]
    h = _rms(x, g_ref[...]) * (1.0 + sc_ref[...]) + sh_ref[...]
    hb = h.astype(BF16)

    def proj(i):
        return jnp.dot(hb, wm_ref[:, i * D_MODEL:(i + 1) * D_MODEL], preferred_element_type=F32)

    qkv_ref[:, 0:D_MODEL] = proj(0).astype(BF16)
    qkv_ref[:, D_MODEL:2 * D_MODEL] = (proj(1) * (M_HEAD_DIM ** -0.5)).astype(BF16)
    qkv_ref[:, 2 * D_MODEL:3 * D_MODEL] = proj(2).astype(BF16)
    mo_ref[...] = proj(3)
    gab_ref[:, 0:D_MODEL] = proj(7)
    gab_ref[:, D_MODEL:2 * D_MODEL] = proj(8)

    zif = jnp.dot(hb, wif_ref[...], preferred_element_type=F32) + bif_ref[...]
    lane = lax.broadcasted_iota(jnp.int32, zif.shape, 1)
    logsig = jnp.minimum(zif, 0.0) - jnp.log1p(jnp.exp(-jnp.abs(zif)))
    gates_ref[...] = jnp.where(lane < M_HEADS, zif, logsig)

    row = lax.broadcasted_iota(jnp.int32, (tb, LANES), 0) + pl.program_id(0) * tb
    pos = (pos0 + row % period).astype(F32)
    ang = pos * inv_ref[...]
    cos = jnp.cos(ang)
    sin = jnp.sin(ang)
    lane = lax.broadcasted_iota(jnp.int32, (tb, LANES), 1)
    sin_signed = jnp.where((lane % 64) < 32, -sin, sin)
    first = lane < A_HEAD_DIM

    dq = proj(4)
    dk = proj(5)
    dv = proj(6)
    vf_ref[...] = dv
    if head_major:
        q_scale = A_HEAD_DIM ** -0.5 * math.log2(math.e)
        extra = lax.broadcasted_iota(jnp.int32, (VT_ROWS - A_VDIM, tb), 0)
        ones_rows = jnp.where(extra == 0, 1.0, 0.0).astype(BF16)
    else:
        q_scale = A_HEAD_DIM ** -0.5
        vb_ref[...] = dv.astype(BF16)
    for hd in range(A_HEADS):
        sl = slice(hd * LANES, (hd + 1) * LANES)
        qr = _rope128(dq[:, sl], cos, sin_signed) * q_scale
        q1 = jnp.where(first, qr, 0.0)
        q2 = jnp.where(first, 0.0, qr)
        kr = _rope128(dk[:, sl], cos, sin_signed)
        kf_ref[:, sl] = kr
        if head_major:
            q1_ref[hd] = q1.T.astype(BF16)
            q2_ref[hd] = q2.T.astype(BF16)
            kb_ref[hd] = kr.astype(BF16)
            vb_ref[hd, 0:A_VDIM, :] = dv[:, sl].T.astype(BF16)
            vb_ref[hd, A_VDIM:VT_ROWS, :] = ones_rows
        else:
            q1_ref[:, sl] = q1.astype(BF16)
            q2_ref[:, sl] = q2.astype(BF16)
            kb_ref[:, sl] = kr.astype(BF16)


def _inproj(x, scale, shift, g_pre, w_main, w_if, b_if128, inv128, *, pos0, period, tb, head_major):
    t = x.shape[0]
    mod_rows = scale.shape[0]
    if mod_rows == 1:
        mod_spec = pl.BlockSpec((1, D_MODEL), lambda i: (0, 0))
    else:
        mod_spec = pl.BlockSpec((tb, D_MODEL), lambda i: (i, 0))
    const = lambda shape: pl.BlockSpec(shape, lambda i: (0, 0))
    tok = lambda w: (pl.BlockSpec((tb, w), lambda i: (i, 0)), (t, w))
    if head_major:
        feat_t = lambda rows: (pl.BlockSpec((A_HEADS, rows, tb), lambda i: (0, 0, i)), (A_HEADS, rows, t))
        tok_h = (pl.BlockSpec((A_HEADS, tb, A_VDIM), lambda i: (0, i, 0)), (A_HEADS, t, A_VDIM))
        q_out, k_out, v_out = feat_t(A_VDIM), tok_h, feat_t(VT_ROWS)
    else:
        q_out = k_out = v_out = tok(D_MODEL)
    outs = [
        (tok(3 * D_MODEL), BF16),
        (tok(D_MODEL), F32),
        (tok(2 * D_MODEL), F32),
        (tok(LANES), F32),
        (q_out, BF16),
        (q_out, BF16),
        (tok(D_MODEL), F32),
        (k_out, BF16),
        (tok(D_MODEL), F32),
        (v_out, BF16),
    ]
    return pl.pallas_call(
        functools.partial(_inproj_kernel, tb=tb, pos0=pos0, period=period, head_major=head_major),
        grid=(t // tb,),
        in_specs=[tok(D_MODEL)[0], mod_spec, mod_spec, const((1, D_MODEL)),
                  const((D_MODEL, N_MAIN_COLS)), const((D_MODEL, LANES)), const((1, LANES)), const((1, LANES))],
        out_specs=[spec for (spec, _), _ in outs],
        out_shape=[jax.ShapeDtypeStruct(shape, dt) for (_, shape), dt in outs],
        compiler_params=_cparams(("parallel",)),
        name="in_proj",
    )(x, scale, shift, g_pre, w_main, w_if, b_if128, inv128)


def _mlstm_kernel(qkv_ref, gates_ref, mo_ref, nw_ref, c0_ref, n0_ref, m0_ref,
                  ya_ref, c_ref, n_ref, m_ref, *, L):
    @pl.when(pl.program_id(1) == 0)
    def _():
        c_ref[...] = c0_ref[...]
        n_ref[...] = n0_ref[...]
        m_ref[...] = m0_ref[...]

    gates = gates_ref[...]
    r = lax.broadcasted_iota(jnp.int32, (L, L), 0)
    c = lax.broadcasted_iota(jnp.int32, (L, L), 1)
    causal = c <= r
    tril = jnp.where(causal, 1.0, 0.0).astype(BF16)
    eye8 = jnp.where(lax.broadcasted_iota(jnp.int32, (8, LANES), 0)
                     == lax.broadcasted_iota(jnp.int32, (8, LANES), 1), 1.0, 0.0).astype(BF16)
    g3 = _split3(gates)
    cums = sum(jnp.dot(tril, gi, preferred_element_type=F32) for gi in g3)
    gt8 = sum(_nt(eye8, gi) for gi in g3)
    bt8 = sum(_nt(eye8, ci) for ci in _split3(cums))

    for h in range(M_HEADS):
        sl = slice(h * M_HEAD_DIM, (h + 1) * M_HEAD_DIM)
        q = qkv_ref[:, h * M_HEAD_DIM:(h + 1) * M_HEAD_DIM]
        k = qkv_ref[:, D_MODEL + h * M_HEAD_DIM:D_MODEL + (h + 1) * M_HEAD_DIM]
        v = qkv_ref[:, 2 * D_MODEL + h * M_HEAD_DIM:2 * D_MODEL + (h + 1) * M_HEAD_DIM]
        ig_col = gates[:, h:h + 1]
        ig_row = gt8[h:h + 1, :]
        b_col = cums[:, M_HEADS + h:M_HEADS + h + 1]
        b_row = bt8[M_HEADS + h:M_HEADS + h + 1, :]
        m_prev = m_ref[0, h:h + 1, 0:1]
        c_old = c_ref[0, h]
        n_old = n_ref[0, h:h + 1, :]

        dmat = jnp.where(causal, b_col - b_row + ig_row, NEG)
        m_inter = b_col + m_prev
        m_t = jnp.maximum(m_inter, jnp.max(dmat, axis=-1, keepdims=True))
        w_inter = jnp.exp(m_inter - m_t)
        a = jnp.exp(dmat - m_t) * _nt(q, k)
        num = w_inter * _nt(q, c_old.astype(BF16)) + jnp.dot(a.astype(BF16), v, preferred_element_type=F32)
        qn = jnp.sum(q.astype(F32) * n_old, axis=-1, keepdims=True)
        den = w_inter * qn + jnp.sum(a, axis=-1, keepdims=True)
        hm = num / jnp.maximum(jnp.abs(den), jnp.exp(-m_t))

        b_last = b_col[L - 1:L, :]
        g_row = b_last - b_row + ig_row
        g_col = b_last - b_col + ig_col
        m_new = jnp.maximum(b_last + m_prev, jnp.max(g_row, axis=-1, keepdims=True))
        dec = jnp.exp(b_last + m_prev - m_new)
        w_col = jnp.exp(g_col - m_new)
        vw = (v.astype(F32) * w_col).astype(BF16)
        c_ref[0, h] = dec * c_old + _tn(vw, k)
        n_ref[0, h:h + 1, :] = dec * n_old + jnp.sum(k.astype(F32) * w_col, axis=0, keepdims=True)
        m_ref[0, h:h + 1, :] = jnp.broadcast_to(m_new, (1, LANES))

        ya_ref[:, sl] = _rms(hm, nw_ref[:, sl]) * jax.nn.sigmoid(mo_ref[:, sl])


def _mlstm(qkv, gates, mo, norm_w, c0, n0, m0, *, batch, L):
    t_total = qkv.shape[0]
    nc = t_total // (batch * L)
    tok = lambda w: pl.BlockSpec((L, w), lambda b, c: (b * nc + c, 0))
    st4 = pl.BlockSpec((1, M_HEADS, M_HEAD_DIM, M_HEAD_DIM), lambda b, c: (b, 0, 0, 0))
    st3n = pl.BlockSpec((1, M_HEADS, M_HEAD_DIM), lambda b, c: (b, 0, 0))
    st3m = pl.BlockSpec((1, M_HEADS, LANES), lambda b, c: (b, 0, 0))
    return pl.pallas_call(
        functools.partial(_mlstm_kernel, L=L),
        grid=(batch, nc),
        in_specs=[tok(3 * D_MODEL), tok(LANES), tok(D_MODEL), pl.BlockSpec((1, D_MODEL), lambda b, c: (0, 0)),
                  st4, st3n, st3m],
        out_specs=[tok(D_MODEL), st4, st3n, st3m],
        out_shape=[jax.ShapeDtypeStruct((t_total, D_MODEL), F32),
                   jax.ShapeDtypeStruct(c0.shape, F32),
                   jax.ShapeDtypeStruct(n0.shape, F32),
                   jax.ShapeDtypeStruct(m0.shape, F32)],
        compiler_params=_cparams(("parallel", "arbitrary")),
        name="mlstm",
    )(qkv, gates, mo, norm_w, c0, n0, m0)


def _lambda_full(lq1_ref, lk1_ref, lq2_ref, lk2_ref):
    a = jnp.exp(jnp.sum(lq1_ref[...] * lk1_ref[...], axis=-1, keepdims=True))
    b = jnp.exp(jnp.sum(lq2_ref[...] * lk2_ref[...], axis=-1, keepdims=True))
    return a - b + LAMBDA_INIT


def _attn_prompt_kernel(qi_ref, kj_ref, q1_ref, q2_ref, k_ref, vt_ref, lq1_ref, lk1_ref, lq2_ref, lk2_ref, dn_ref,
                        o_ref, m_sc, a_sc, acc_sc, s_sc, p_sc, *, tq):
    step = pl.program_id(0)
    qi = qi_ref[step]
    kj = kj_ref[step]
    n_sub = k_ref.shape[1] // tq
    last_kj = qi // n_sub
    q_refs = (q1_ref, q2_ref)

    @pl.when(kj == 0)
    def _():
        m_sc[...] = jnp.full(m_sc.shape, NEG, F32)
        acc_sc[...] = jnp.zeros(acc_sc.shape, F32)

    def run(subs):
        n = 2 * len(subs)

        def scores(head, g):
            sub = subs[g // 2][0]
            s_sc[g] = jnp.dot(k_ref[head, sub * tq:(sub + 1) * tq, :], q_refs[g % 2][head],
                              preferred_element_type=F32)

        def softmax(head, g):
            pair = 2 * head + g % 2
            s = s_sc[g]
            if subs[g // 2][1]:
                key = lax.broadcasted_iota(jnp.int32, (tq, tq), 0)
                qry = lax.broadcasted_iota(jnp.int32, (tq, tq), 1)
                s = jnp.where(key <= qry, s, NEG)
            m_prev = m_sc[pair]
            m_new = jnp.maximum(m_prev, jnp.max(s, axis=0, keepdims=True))
            a_sc[g] = jnp.exp2(m_prev - m_new)
            p_sc[g] = jnp.exp2(s - m_new).astype(BF16)
            m_sc[pair] = m_new

        def values(head, g):
            pair = 2 * head + g % 2
            sub = subs[g // 2][0]
            pv = jnp.dot(vt_ref[head, :, sub * tq:(sub + 1) * tq], p_sc[g], preferred_element_type=F32)
            acc_sc[pair] = a_sc[g] * acc_sc[pair] + pv

        for g in range(n):
            scores(0, g)
        softmax(0, 0)

        def body(head, carry):
            for g in range(n):
                values(head, g)
                if g + 1 < n:
                    softmax(head, g + 1)
                else:
                    softmax(head + 1, 0)
                scores(head + 1, g)
            return carry

        lax.fori_loop(0, A_HEADS - 1, body, 0)
        for g in range(n):
            values(A_HEADS - 1, g)
            if g + 1 < n:
                softmax(A_HEADS - 1, g + 1)

    @pl.when(kj < last_kj)
    def _():
        run(tuple((sub, False) for sub in range(n_sub)))

    for diag in range(n_sub):
        @pl.when((kj == last_kj) & (qi % n_sub == diag))
        def _(diag=diag):
            run(tuple((sub, False) for sub in range(diag)) + ((diag, True),))

    @pl.when(kj == last_kj)
    def _():
        lam = _lambda_full(lq1_ref, lk1_ref, lq2_ref, lk2_ref)
        for h in range(A_HEADS):
            a1 = acc_sc[2 * h]
            a2 = acc_sc[2 * h + 1]
            ot = (a1[0:A_VDIM] / a1[A_VDIM:A_VDIM + 1]
                  - lam * (a2[0:A_VDIM] / a2[A_VDIM:A_VDIM + 1]))
            o_ref[:, h * LANES:(h + 1) * LANES] = _rms(ot.T, dn_ref[...]) * (1.0 - LAMBDA_INIT)


def _attn_prompt(q1t, q2t, kh, vt, lams, diff_norm, *, tq, n_sub):
    t = kh.shape[1]
    nq = t // tq
    tk = n_sub * tq
    assert t % tk == 0
    steps = [(qi, kj) for qi in range(nq) for kj in range(qi // n_sub + 1)]
    qi_of = jnp.asarray([s[0] for s in steps], jnp.int32)
    kj_of = jnp.asarray([s[1] for s in steps], jnp.int32)
    qspec = pl.BlockSpec((A_HEADS, A_VDIM, tq), lambda s, qi, kj: (0, 0, qi[s]))
    kspec = pl.BlockSpec((A_HEADS, tk, A_VDIM), lambda s, qi, kj: (0, kj[s], 0))
    vspec = pl.BlockSpec((A_HEADS, VT_ROWS, tk), lambda s, qi, kj: (0, 0, kj[s]))
    small = lambda n: pl.BlockSpec((1, n), lambda s, qi, kj: (0, 0))
    grid_spec = pltpu.PrefetchScalarGridSpec(
        num_scalar_prefetch=2,
        grid=(len(steps),),
        in_specs=[qspec, qspec, kspec, vspec] + [small(A_HEAD_DIM)] * 4 + [small(A_VDIM)],
        out_specs=pl.BlockSpec((tq, D_MODEL), lambda s, qi, kj: (qi[s], 0)),
        scratch_shapes=[pltpu.VMEM((2 * A_HEADS, 1, tq), F32),
                        pltpu.VMEM((2 * n_sub, 1, tq), F32),
                        pltpu.VMEM((2 * A_HEADS, VT_ROWS, tq), F32),
                        pltpu.VMEM((2 * n_sub, tq, tq), F32), pltpu.VMEM((2 * n_sub, tq, tq), BF16)],
    )
    return pl.pallas_call(
        functools.partial(_attn_prompt_kernel, tq=tq),
        grid_spec=grid_spec,
        out_shape=jax.ShapeDtypeStruct((t, D_MODEL), F32),
        compiler_params=_cparams(("arbitrary",)),
        name="diff_attn_prompt",
    )(qi_of, kj_of, q1t, q2t, kh, vt, *lams, diff_norm)


def _attn_sample_kernel(pt_ref, q_ref, kn_ref, vn_ref, lq1_ref, lk1_ref, lq2_ref, lk2_ref, dn_ref, *rest,
                        ppb, n_new, page):
    k_refs = rest[:ppb]
    v_refs = rest[ppb:2 * ppb]
    o_ref, m_sc, l_sc, acc_sc = rest[2 * ppb:]
    j = pl.program_id(1)
    rows = q_ref.shape[2]

    @pl.when(j == 0)
    def _():
        m_sc[...] = jnp.full(m_sc.shape, NEG, F32)
        l_sc[...] = jnp.zeros(l_sc.shape, F32)
        acc_sc[...] = jnp.zeros(acc_sc.shape, F32)

    def update(scores, values):
        s = jnp.concatenate([scores(h) for h in range(A_HEADS)], axis=0)
        m_prev = m_sc[...]
        m_new = jnp.maximum(m_prev, jnp.max(s, axis=-1, keepdims=True))
        alpha = jnp.exp(m_prev - m_new)
        p = jnp.exp(s - m_new)
        l_sc[...] = alpha * l_sc[...] + jnp.sum(p, axis=-1, keepdims=True)
        pb = p.astype(BF16)
        pv = jnp.concatenate([jnp.dot(pb[h * rows:(h + 1) * rows], values(h), preferred_element_type=F32)
                              for h in range(A_HEADS)], axis=0)
        acc_sc[...] = alpha * acc_sc[...] + pv
        m_sc[...] = m_new

    def head_rows(refs, h):
        parts = [r[0, pl.ds(h, page, stride=A_HEADS), :] for r in refs]
        return jnp.concatenate(parts, axis=0).astype(BF16)

    update(lambda h: _nt(q_ref[0, h], head_rows(k_refs, h)), lambda h: head_rows(v_refs, h))

    @pl.when(j == pl.num_programs(1) - 1)
    def _():
        lam = _lambda_full(lq1_ref, lk1_ref, lq2_ref, lk2_ref)
        r = lax.broadcasted_iota(jnp.int32, (rows, rows), 0)
        c = lax.broadcasted_iota(jnp.int32, (rows, rows), 1)
        keep = (c <= r % n_new) & (c < n_new)
        update(lambda h: jnp.where(keep, _nt(q_ref[0, h], kn_ref[0, h]), NEG), lambda h: vn_ref[0, h])
        o = acc_sc[...] / l_sc[...]
        for h in range(A_HEADS):
            od = o[h * rows:h * rows + n_new, :] - lam * o[h * rows + n_new:h * rows + 2 * n_new, :]
            o_ref[0, :, h * LANES:(h + 1) * LANES] = _rms(od, dn_ref[...]) * (1.0 - LAMBDA_INIT)


def _attn_sample(q_rows, k_new, v_new, cache_k, cache_v, page_table, lams, diff_norm, *, ppb, n_new, page_base):
    bsz, n_pages = page_table.shape
    page = cache_k.shape[1] // A_HEADS
    rows = q_rows.shape[2]
    seq = lambda: pl.BlockSpec((1, A_HEADS, rows, A_VDIM), lambda b, j, pt: (b, 0, 0, 0))
    small = lambda n: pl.BlockSpec((1, n), lambda b, j, pt: (0, 0))

    def page_spec(i):
        return pl.BlockSpec((1, page * A_HEADS, A_VDIM), lambda b, j, pt: (page_base + pt[b, j * ppb + i], 0, 0))

    grid_spec = pltpu.PrefetchScalarGridSpec(
        num_scalar_prefetch=1,
        grid=(bsz, n_pages // ppb),
        in_specs=[seq(), seq(), seq()] + [small(A_HEAD_DIM)] * 4 + [small(A_VDIM)]
                 + [page_spec(i) for i in range(ppb)] + [page_spec(i) for i in range(ppb)],
        out_specs=pl.BlockSpec((1, n_new, D_MODEL), lambda b, j, pt: (b, 0, 0)),
        scratch_shapes=[pltpu.VMEM((A_HEADS * rows, 1), F32),
                        pltpu.VMEM((A_HEADS * rows, 1), F32),
                        pltpu.VMEM((A_HEADS * rows, A_VDIM), F32)],
    )
    return pl.pallas_call(
        functools.partial(_attn_sample_kernel, ppb=ppb, n_new=n_new, page=page),
        grid_spec=grid_spec,
        out_shape=jax.ShapeDtypeStruct((bsz, n_new, D_MODEL), F32),
        compiler_params=_cparams(("parallel", "arbitrary")),
        name="diff_attn_sample",
    )(page_table, q_rows, k_new, v_new, *lams, diff_norm, *([cache_k] * ppb), *([cache_v] * ppb))


def _postmix_kernel(x_ref, ya_ref, yb_ref, gab_ref, g1_ref, sc2_ref, sh2_ref, gpost_ref, gpre_ref, wo_ref, wq_ref,
                    x1_ref, h2_ref, qp_ref):
    merged = (jax.nn.sigmoid(gab_ref[:, 0:D_MODEL]) * ya_ref[...]
              + jax.nn.sigmoid(gab_ref[:, D_MODEL:2 * D_MODEL]) * yb_ref[...])
    o = jnp.dot(merged.astype(BF16), wo_ref[...], preferred_element_type=F32)
    x1 = x_ref[...] + g1_ref[...] * _rms(o, gpost_ref[...])
    x1_ref[...] = x1
    h2 = (_rms(x1, gpre_ref[...]) * (1.0 + sc2_ref[...]) + sh2_ref[...]).astype(BF16)
    h2_ref[...] = h2
    qp_ref[...] = jnp.dot(h2, wq_ref[...], preferred_element_type=F32).astype(BF16)


def _postmix(x, ya, yb, gab, gate1, scale2, shift2, g_post1, g_pre2, w_out, w_q, *, tb):
    t = x.shape[0]
    nq = w_q.shape[1]
    if gate1.shape[0] == 1:
        mod_spec = pl.BlockSpec((1, D_MODEL), lambda i: (0, 0))
    else:
        mod_spec = pl.BlockSpec((tb, D_MODEL), lambda i: (i, 0))
    const = lambda shape: pl.BlockSpec(shape, lambda i: (0, 0))
    tok = lambda w: pl.BlockSpec((tb, w), lambda i: (i, 0))
    return pl.pallas_call(
        _postmix_kernel,
        grid=(t // tb,),
        in_specs=[tok(D_MODEL), tok(D_MODEL), tok(D_MODEL), tok(2 * D_MODEL), mod_spec, mod_spec, mod_spec,
                  const((1, D_MODEL)), const((1, D_MODEL)), const((D_MODEL, D_MODEL)), const((D_MODEL, nq))],
        out_specs=[tok(D_MODEL), tok(D_MODEL), tok(nq)],
        out_shape=[jax.ShapeDtypeStruct((t, D_MODEL), F32),
                   jax.ShapeDtypeStruct((t, D_MODEL), BF16),
                   jax.ShapeDtypeStruct((t, nq), BF16)],
        compiler_params=_cparams(("parallel",)),
        name="post_mix",
    )(x, ya, yb, gab, gate1, scale2, shift2, g_post1, g_pre2, w_out, w_q)


NOT_TOP = 255.0


def _top_ranked(s, n):
    vals = []
    work = s
    rank = jnp.full(s.shape, NOT_TOP, F32)
    for i in range(n):
        mx = jnp.max(work, axis=0, keepdims=True)
        hit = work == mx
        rank = jnp.where(hit, float(i), rank)
        work = jnp.where(hit, NEG, work)
        vals.append(mx)
    return vals, rank


def _peer_score_kernel(qp_ref, k1_ref, k2_ref, c1_ref, e1_ref, r2_ref, e2_ref, *, tt):
    ab = N_KEYS // 8
    jrow = lax.broadcasted_iota(jnp.int32, (8, tt), 0)
    for h in range(PEER_HEADS):
        base = h * 2 * N_KEYS
        s1 = _nt(k1_ref[h], qp_ref[:, base:base + N_KEYS])
        s2 = _nt(k2_ref[h], qp_ref[:, base + N_KEYS:base + 2 * N_KEYS])
        t1, _ = _top_ranked(s1, PEER_TOPK)
        t2, rank2 = _top_ranked(s2, PEER_TOPK)
        t2a = jnp.concatenate(t2[:8], axis=0)
        t2b = jnp.concatenate(t2[8:], axis=0)
        slabs = [t1[0] + t2a, t1[0] + t2b]
        for i in range(1, PEER_TOPK):
            slabs.append(jnp.where(jrow < PEER_TOPK // (i + 1), t1[i] + t2a, NEG))
        cand = jnp.concatenate(slabs, axis=0)
        tau = _top_ranked(cand, PEER_TOPK)[0][-1]
        top = t1[0] + t2[0]
        sel = cand >= tau
        z = jnp.sum(jnp.where(sel, jnp.exp(cand - top), 0.0), axis=0, keepdims=True)
        ones = jnp.where(sel, 1.0, 0.0)
        counts = [jnp.sum(ones[0:16], axis=0, keepdims=True)]
        for i in range(1, PEER_TOPK):
            counts.append(jnp.sum(ones[8 + 8 * i:16 + 8 * i], axis=0, keepdims=True))
        c1 = jnp.zeros(s1.shape, F32)
        for i in range(PEER_TOPK):
            c1 = jnp.where(s1 == t1[i], counts[i], c1)
        c1_ref[:, h] = c1.reshape(ab, 8, tt)
        e1_ref[:, h] = (jnp.exp(s1 - t1[0]) * (0.5 / z)).reshape(ab, 8, tt)
        r2_ref[h] = rank2.astype(BF16)
        e2_ref[h] = jnp.exp(s2 - t2[0]).astype(BF16)


def _peer_scores(qp, k1, k2, *, tt):
    t = qp.shape[0]
    ab = N_KEYS // 8
    a_spec = pl.BlockSpec((ab, PEER_HEADS, 8, tt), lambda i: (0, 0, 0, i))
    b_spec = pl.BlockSpec((PEER_HEADS, N_KEYS, tt), lambda i: (0, 0, i))
    kspec = pl.BlockSpec((PEER_HEADS, N_KEYS, N_KEYS), lambda i: (0, 0, 0))
    a_shape = jax.ShapeDtypeStruct((ab, PEER_HEADS, 8, t), F32)
    b_shape = jax.ShapeDtypeStruct((PEER_HEADS, N_KEYS, t), BF16)
    return pl.pallas_call(
        functools.partial(_peer_score_kernel, tt=tt),
        grid=(t // tt,),
        in_specs=[pl.BlockSpec((tt, qp.shape[1]), lambda i: (i, 0)), kspec, kspec],
        out_specs=[a_spec, a_spec, b_spec, b_spec],
        out_shape=[a_shape, a_shape, b_shape, b_shape],
        compiler_params=_cparams(("parallel",)),
        name="peer_scores",
    )(qp, k1, k2)


def _gelu_x2(x):
    return x * (1.0 + lax.erf(x * (2.0 ** -0.5)))


def _peer_dense_kernel(h2_ref, u_ref, v_ref, c1_ref, e1_ref, r2_ref, e2_ref, x1_ref, g2_ref, gpost_ref,
                       y_ref, acc_sc, a_sc, *, a_per):
    j = pl.program_id(1)

    @pl.when(j == 0)
    def _():
        acc_sc[...] = jnp.zeros(acc_sc.shape, F32)

    zt = _nt(u_ref[...], h2_ref[...])
    zero = jnp.zeros((), BF16)
    tt = h2_ref.shape[0]

    def key_rows(ref, al, h):
        row = ref[al // 8, h, al % 8:al % 8 + 1, :]
        tile = jnp.broadcast_to(row, (16, tt)).astype(BF16)
        return jnp.tile(tile, (N_KEYS // 16, 1))

    for al in range(a_per):
        w = None
        for h in range(PEER_HEADS):
            term = jnp.where(r2_ref[h] < key_rows(c1_ref, al, h), e2_ref[h] * key_rows(e1_ref, al, h), zero)
            w = term if w is None else w + term
        rows = slice(al * N_KEYS, (al + 1) * N_KEYS)
        a_sc[rows, :] = w * _gelu_x2(zt[rows, :]).astype(BF16)
    acc_sc[...] += _tn(v_ref[...], a_sc[...])

    @pl.when(j == pl.num_programs(1) - 1)
    def _():
        f = acc_sc[...].T
        y_ref[...] = x1_ref[...] + g2_ref[...] * _rms(f, gpost_ref[...])


def _peer_dense(h2, u_b, v_b, c1, e1, r2, e2, x1, gate2, g_post2, *, tt, a_per):
    t = h2.shape[0]
    n_exp = u_b.shape[0]
    eb = a_per * N_KEYS
    if gate2.shape[0] == 1:
        mod_spec = pl.BlockSpec((1, D_MODEL), lambda i, j: (0, 0))
    else:
        mod_spec = pl.BlockSpec((tt, D_MODEL), lambda i, j: (i, 0))
    a_spec = pl.BlockSpec((a_per // 8, PEER_HEADS, 8, tt), lambda i, j: (j, 0, 0, i))
    b_spec = pl.BlockSpec((PEER_HEADS, N_KEYS, tt), lambda i, j: (0, 0, i))
    return pl.pallas_call(
        functools.partial(_peer_dense_kernel, a_per=a_per),
        grid=(t // tt, n_exp // eb),
        in_specs=[pl.BlockSpec((tt, D_MODEL), lambda i, j: (i, 0)),
                  pl.BlockSpec((eb, D_MODEL), lambda i, j: (j, 0)),
                  pl.BlockSpec((eb, D_MODEL), lambda i, j: (j, 0)),
                  a_spec, a_spec, b_spec, b_spec,
                  pl.BlockSpec((tt, D_MODEL), lambda i, j: (i, 0)),
                  mod_spec,
                  pl.BlockSpec((1, D_MODEL), lambda i, j: (0, 0))],
        out_specs=pl.BlockSpec((tt, D_MODEL), lambda i, j: (i, 0)),
        out_shape=jax.ShapeDtypeStruct((t, D_MODEL), F32),
        scratch_shapes=[pltpu.VMEM((D_MODEL, tt), F32), pltpu.VMEM((eb, tt), BF16)],
        compiler_params=_cparams(("parallel", "arbitrary")),
        name="peer_dense",
    )(h2, u_b, v_b, c1, e1, r2, e2, x1, gate2, g_post2)


def _block(t, pref):
    return pref if t % pref == 0 else t


def _layer(x, mods, weights, *, pos0, period, run_mlstm, attend, head_major):
    shift1, scale1, gate1, shift2, scale2, gate2 = mods
    w = weights
    t = x.shape[0]
    tb = _block(t, 256)
    qkv, mo, gab, gates, q1, q2, kf, kb, vf, vb = _inproj(
        x, scale1, shift1, w["g_pre1"], w["w_main"], w["w_if"], w["b_if128"], w["inv128"],
        pos0=pos0, period=period, tb=tb, head_major=head_major)

    ya, c_new, n_new, m_new = run_mlstm(qkv, gates, mo)
    yb = attend(q1, q2, kb, vb)

    x1, h2, qp = _postmix(x, ya, yb, gab, gate1, scale2, shift2, w["g_post1"], w["g_pre2"], w["w_out"], w["w_q"], tb=tb)
    tt = _block(t, 256)
    c1, e1, r2, e2 = _peer_scores(qp, w["k1"], w["k2"], tt=tt)
    ttd = _block(t, 1024)
    y = _peer_dense(h2, w["u"], w["v"], c1, e1, r2, e2, x1, gate2, w["g_post2"], tt=ttd, a_per=8)
    return y, kf, vf, c_new, n_new, m_new


def _hybrid_step(x_prompt, x_sample, cache_k, cache_v, state_C, state_n, state_m, page_table, c_prompt, c_sample,
                 w_ada, b_ada, g_pre1, g_post1, g_pre2, g_post2, w_in, b_if, mlstm_norm,
                 lambda_q1, lambda_k1, lambda_q2, lambda_k2, diff_norm, w_out,
                 peer_wq, peer_k1, peer_k2, peer_u, peer_v, *, m_chunk=128, tq=512, attn_sub=4, ppb=16):
    l = 0
    bp, sp, _ = x_prompt.shape
    bs, ts, _ = x_sample.shape
    assert bp == 1
    row = lambda a: a.reshape(1, -1)

    wi = w_in[l]
    n_if = 2 * M_HEADS
    w_main = jnp.concatenate([wi[:, :4 * D_MODEL], wi[:, 4 * D_MODEL + n_if:]], axis=1).astype(BF16)
    w_if = jnp.pad(wi[:, 4 * D_MODEL:4 * D_MODEL + n_if], ((0, 0), (0, LANES - n_if))).astype(BF16)
    b_if128 = jnp.pad(b_if[l], (0, LANES - n_if)).reshape(1, LANES)
    inv = ROPE_THETA ** (-jnp.arange(0, A_HEAD_DIM, 2, dtype=F32) / A_HEAD_DIM)
    inv128 = jnp.tile(inv, LANES // inv.shape[0]).reshape(1, LANES)
    weights = dict(
        g_pre1=row(g_pre1[l]), g_post1=row(g_post1[l]), g_pre2=row(g_pre2[l]), g_post2=row(g_post2[l]),
        w_main=w_main, w_if=w_if, b_if128=b_if128, inv128=inv128,
        w_out=w_out[l].astype(BF16), w_q=peer_wq[l].astype(BF16),
        k1=peer_k1[l].astype(BF16), k2=peer_k2[l].astype(BF16),
        u=peer_u[l].astype(BF16), v=peer_v[l].astype(BF16))
    lams = [row(a[l]) for a in (lambda_q1, lambda_k1, lambda_q2, lambda_k2)]
    dn = row(diff_norm[l])
    nw = row(mlstm_norm[l])

    n_c = bp + bs
    c_all = jnp.pad(jnp.concatenate([c_prompt, c_sample], axis=0), ((0, (-n_c) % 8), (0, 0)))
    mod = _ada(c_all, w_ada[l], b_ada[l])
    mods_p = [mod[0:1, i * D_MODEL:(i + 1) * D_MODEL] for i in range(6)]
    mods_s = [jnp.repeat(mod[bp:n_c, i * D_MODEL:(i + 1) * D_MODEL], ts, axis=0) for i in range(6)]

    def run_mlstm_p(qkv, gates, mo):
        c0 = jnp.zeros((bp, M_HEADS, M_HEAD_DIM, M_HEAD_DIM), F32)
        n0 = jnp.zeros((bp, M_HEADS, M_HEAD_DIM), F32)
        m0 = jnp.zeros((bp, M_HEADS, LANES), F32)
        return _mlstm(qkv, gates, mo, nw, c0, n0, m0, batch=bp, L=math.gcd(sp, m_chunk))

    def attend_p(q1t, q2t, kh, vt):
        tq_p = _block(sp, tq)
        return _attn_prompt(q1t, q2t, kh, vt, lams, dn, tq=tq_p, n_sub=math.gcd(sp // tq_p, attn_sub))

    yp, kp, vp, cp, n_p, mp = _layer(
        x_prompt.reshape(sp, D_MODEL), mods_p, weights, pos0=0, period=sp,
        run_mlstm=run_mlstm_p, attend=attend_p, head_major=True)

    lpad = 16
    past_len = page_table.shape[1] * cache_k.shape[2]

    def pad_tokens(a, fill_row=None):
        a3 = a.reshape(bs, ts, a.shape[-1])
        if fill_row is None:
            out = jnp.pad(a3, ((0, 0), (0, lpad - ts), (0, 0)))
        else:
            fill = jnp.broadcast_to(fill_row.astype(a.dtype), (bs, lpad - ts, a.shape[-1]))
            out = jnp.concatenate([a3, fill], axis=1)
        return out.reshape(bs * lpad, a.shape[-1])

    def run_mlstm_s(qkv, gates, mo):
        lane = jnp.arange(LANES)
        gate_fill = jnp.where(lane < M_HEADS, NEG, 0.0).astype(F32)
        m0 = jnp.broadcast_to(state_m[l][:, :, None], (bs, M_HEADS, LANES))
        ya, c_new, n_new, m_new = _mlstm(pad_tokens(qkv), pad_tokens(gates, gate_fill), pad_tokens(mo), nw,
                                         state_C[l], state_n[l], m0, batch=bs, L=lpad)
        ya = ya.reshape(bs, lpad, D_MODEL)[:, :ts].reshape(bs * ts, D_MODEL)
        return ya, c_new, n_new, m_new

    def attend_s(q1, q2, kb, vb):
        heads = lambda a: a.reshape(bs, ts, A_HEADS, A_VDIM).transpose(0, 2, 1, 3)
        zeros = lambda n: jnp.zeros((bs, A_HEADS, n, A_VDIM), BF16)
        q_rows = jnp.concatenate([heads(q1), heads(q2), zeros(lpad - 2 * ts)], axis=2)
        k_new = jnp.concatenate([heads(kb), zeros(lpad - ts)], axis=2)
        v_new = jnp.concatenate([heads(vb), zeros(lpad - ts)], axis=2)
        depth, n_pool, page = cache_k.shape[0], cache_k.shape[1], cache_k.shape[2]
        ck = cache_k.reshape(depth * n_pool, page * A_HEADS, A_VDIM)
        cv = cache_v.reshape(depth * n_pool, page * A_HEADS, A_VDIM)
        o = _attn_sample(q_rows, k_new, v_new, ck, cv, page_table, lams, dn,
                         ppb=math.gcd(page_table.shape[1], ppb), n_new=ts, page_base=l * n_pool)
        return o.reshape(bs * ts, D_MODEL)

    ys, k_s, v_s, cs, n_s, ms = _layer(
        x_sample.reshape(bs * ts, D_MODEL), mods_s, weights, pos0=past_len, period=ts,
        run_mlstm=run_mlstm_s, attend=attend_s, head_major=False)

    return (yp.reshape(bp, sp, D_MODEL), ys.reshape(bs, ts, D_MODEL),
            kp.reshape(1, bp, sp, A_HEADS, A_VDIM), vp.reshape(1, bp, sp, A_HEADS, A_VDIM),
            cp[None], n_p[None], mp[None, :, :, 0],
            k_s.reshape(1, bs, ts, A_HEADS, A_VDIM), v_s.reshape(1, bs, ts, A_HEADS, A_VDIM),
            cs[None], n_s[None], ms[None, :, :, 0])


def kernel(x_prompt, x_sample, cache_k, cache_v, state_C, state_n, state_m, page_table, c_prompt, c_sample, w_ada, b_ada, g_pre1, g_post1, g_pre2, g_post2, w_in, b_if, mlstm_norm, lambda_q1, lambda_k1, lambda_q2, lambda_k2, diff_norm, w_out, peer_wq, peer_k1, peer_k2, peer_u, peer_v):
    return _hybrid_step(x_prompt, x_sample, cache_k, cache_v, state_C, state_n, state_m, page_table, c_prompt, c_sample,
                        w_ada, b_ada, g_pre1, g_post1, g_pre2, g_post2, w_in, b_if, mlstm_norm,
                        lambda_q1, lambda_k1, lambda_q2, lambda_k2, diff_norm, w_out,
                        peer_wq, peer_k1, peer_k2, peer_u, peer_v)
```

```python
import functools
import math

import jax
import jax.numpy as jnp
from jax import lax
from jax.experimental import pallas as pl
from jax.experimental.pallas import tpu as pltpu

F32 = jnp.float32
BF16 = jnp.bfloat16

D_MODEL = 1024
M_HEADS = 4
M_HEAD_DIM = 256
A_HEADS = 8
A_HEAD_DIM = 64
A_VDIM = 128
ROPE_THETA = 10000.0
N_KEYS = 128
PEER_HEADS = 8
PEER_TOPK = 16
NORM_EPS = 1e-6
LAMBDA_INIT = 0.8 - 0.6 * math.exp(-0.3 * 0)
N_MAIN_COLS = 9 * D_MODEL

VT_ROWS = A_VDIM + 16
LANES = 128
NEG = -1e30
VMEM_LIMIT = 56 * 1024 * 1024


def _cparams(sem, flags=None):
    return pltpu.CompilerParams(dimension_semantics=sem, vmem_limit_bytes=VMEM_LIMIT, flags=flags)


def _nt(a, b):
    return lax.dot_general(a, b, (((1,), (1,)), ((), ())), preferred_element_type=F32)


def _tn(a, b):
    return lax.dot_general(a, b, (((0,), (0,)), ((), ())), preferred_element_type=F32)


def _rms(x, g):
    return x * lax.rsqrt(jnp.mean(x * x, axis=-1, keepdims=True) + NORM_EPS) * g


def _split3(x):
    x1 = x.astype(BF16)
    r1 = x - x1.astype(F32)
    x2 = r1.astype(BF16)
    x3 = (r1 - x2.astype(F32)).astype(BF16)
    return x1, x2, x3


def _ada_kernel(c_ref, w_ref, b_ref, o_ref):
    c = c_ref[...]
    s = c * jax.nn.sigmoid(c)
    o_ref[...] = jnp.dot(s.astype(BF16), w_ref[...].astype(BF16), preferred_element_type=F32) + b_ref[...]


def _ada(c_all, w_ada, b_ada):
    r = c_all.shape[0]
    n = w_ada.shape[1]
    bn = D_MODEL
    return pl.pallas_call(
        _ada_kernel,
        grid=(n // bn,),
        in_specs=[pl.BlockSpec((r, D_MODEL), lambda j: (0, 0)),
                  pl.BlockSpec((D_MODEL, bn), lambda j: (0, j)),
                  pl.BlockSpec((1, bn), lambda j: (0, j))],
        out_specs=pl.BlockSpec((r, bn), lambda j: (0, j)),
        out_shape=jax.ShapeDtypeStruct((r, n), F32),
        compiler_params=_cparams(("arbitrary",)),
        name="ada_mod",
    )(c_all, w_ada, b_ada.reshape(1, n))


def _rope128(x, cos, sin_signed):
    fwd = pltpu.roll(x, LANES - 32, axis=1)
    bwd = pltpu.roll(x, 32, axis=1)
    lane = lax.broadcasted_iota(jnp.int32, x.shape, 1)
    rot = jnp.where((lane % 64) < 32, fwd, bwd)
    return x * cos + rot * sin_signed


def _inproj_kernel(x_ref, sc_ref, sh_ref, g_ref, wm_ref, wif_ref, bif_ref, inv_ref,
                   qkv_ref, mo_ref, gab_ref, gates_ref, q1_ref, q2_ref, kf_ref, kb_ref, vf_ref, vb_ref,
                   *, tb, pos0, period, head_major):
    x = x_ref[...]
    h = _rms(x, g_ref[...]) * (1.0 + sc_ref[...]) + sh_ref[...]
    hb = h.astype(BF16)

    def proj(i):
        return jnp.dot(hb, wm_ref[:, i * D_MODEL:(i + 1) * D_MODEL], preferred_element_type=F32)

    qkv_ref[:, 0:D_MODEL] = proj(0).astype(BF16)
    qkv_ref[:, D_MODEL:2 * D_MODEL] = (proj(1) * (M_HEAD_DIM ** -0.5)).astype(BF16)
    qkv_ref[:, 2 * D_MODEL:3 * D_MODEL] = proj(2).astype(BF16)
    mo_ref[...] = proj(3)
    gab_ref[:, 0:D_MODEL] = proj(7)
    gab_ref[:, D_MODEL:2 * D_MODEL] = proj(8)

    zif = jnp.dot(hb, wif_ref[...], preferred_element_type=F32) + bif_ref[...]
    lane = lax.broadcasted_iota(jnp.int32, zif.shape, 1)
    logsig = jnp.minimum(zif, 0.0) - jnp.log1p(jnp.exp(-jnp.abs(zif)))
    gates_ref[...] = jnp.where(lane < M_HEADS, zif, logsig)

    row = lax.broadcasted_iota(jnp.int32, (tb, LANES), 0) + pl.program_id(0) * tb
    pos = (pos0 + row % period).astype(F32)
    ang = pos * inv_ref[...]
    cos = jnp.cos(ang)
    sin = jnp.sin(ang)
    lane = lax.broadcasted_iota(jnp.int32, (tb, LANES), 1)
    sin_signed = jnp.where((lane % 64) < 32, -sin, sin)
    first = lane < A_HEAD_DIM

    dq = proj(4)
    dk = proj(5)
    dv = proj(6)
    vf_ref[...] = dv
    if head_major:
        q_scale = A_HEAD_DIM ** -0.5 * math.log2(math.e)
        extra = lax.broadcasted_iota(jnp.int32, (VT_ROWS - A_VDIM, tb), 0)
        ones_rows = jnp.where(extra == 0, 1.0, 0.0).astype(BF16)
    else:
        q_scale = A_HEAD_DIM ** -0.5
        vb_ref[...] = dv.astype(BF16)
    for hd in range(A_HEADS):
        sl = slice(hd * LANES, (hd + 1) * LANES)
        qr = _rope128(dq[:, sl], cos, sin_signed) * q_scale
        q1 = jnp.where(first, qr, 0.0)
        q2 = jnp.where(first, 0.0, qr)
        kr = _rope128(dk[:, sl], cos, sin_signed)
        kf_ref[:, sl] = kr
        if head_major:
            q1_ref[hd] = q1.T.astype(BF16)
            q2_ref[hd] = q2.T.astype(BF16)
            kb_ref[hd] = kr.astype(BF16)
            vb_ref[hd, 0:A_VDIM, :] = dv[:, sl].T.astype(BF16)
            vb_ref[hd, A_VDIM:VT_ROWS, :] = ones_rows
        else:
            q1_ref[:, sl] = q1.astype(BF16)
            q2_ref[:, sl] = q2.astype(BF16)
            kb_ref[:, sl] = kr.astype(BF16)


def _inproj(x, scale, shift, g_pre, w_main, w_if, b_if128, inv128, *, pos0, period, tb, head_major):
    t = x.shape[0]
    mod_rows = scale.shape[0]
    if mod_rows == 1:
        mod_spec = pl.BlockSpec((1, D_MODEL), lambda i: (0, 0))
    else:
        mod_spec = pl.BlockSpec((tb, D_MODEL), lambda i: (i, 0))
    const = lambda shape: pl.BlockSpec(shape, lambda i: (0, 0))
    tok = lambda w: (pl.BlockSpec((tb, w), lambda i: (i, 0)), (t, w))
    if head_major:
        feat_t = lambda rows: (pl.BlockSpec((A_HEADS, rows, tb), lambda i: (0, 0, i)), (A_HEADS, rows, t))
        tok_h = (pl.BlockSpec((A_HEADS, tb, A_VDIM), lambda i: (0, i, 0)), (A_HEADS, t, A_VDIM))
        q_out, k_out, v_out = feat_t(A_VDIM), tok_h, feat_t(VT_ROWS)
    else:
        q_out = k_out = v_out = tok(D_MODEL)
    outs = [
        (tok(3 * D_MODEL), BF16),
        (tok(D_MODEL), F32),
        (tok(2 * D_MODEL), F32),
        (tok(LANES), F32),
        (q_out, BF16),
        (q_out, BF16),
        (tok(D_MODEL), F32),
        (k_out, BF16),
        (tok(D_MODEL), F32),
        (v_out, BF16),
    ]
    return pl.pallas_call(
        functools.partial(_inproj_kernel, tb=tb, pos0=pos0, period=period, head_major=head_major),
        grid=(t // tb,),
        in_specs=[tok(D_MODEL)[0], mod_spec, mod_spec, const((1, D_MODEL)),
                  const((D_MODEL, N_MAIN_COLS)), const((D_MODEL, LANES)), const((1, LANES)), const((1, LANES))],
        out_specs=[spec for (spec, _), _ in outs],
        out_shape=[jax.ShapeDtypeStruct(shape, dt) for (_, shape), dt in outs],
        compiler_params=_cparams(("parallel",)),
        name="in_proj",
    )(x, scale, shift, g_pre, w_main, w_if, b_if128, inv128)


def _mlstm_kernel(qkv_ref, gates_ref, mo_ref, nw_ref, c0_ref, n0_ref, m0_ref,
                  ya_ref, c_ref, n_ref, m_ref, *, L):
    @pl.when(pl.program_id(1) == 0)
    def _():
        c_ref[...] = c0_ref[...]
        n_ref[...] = n0_ref[...]
        m_ref[...] = m0_ref[...]

    gates = gates_ref[...]
    r = lax.broadcasted_iota(jnp.int32, (L, L), 0)
    c = lax.broadcasted_iota(jnp.int32, (L, L), 1)
    causal = c <= r
    tril = jnp.where(causal, 1.0, 0.0).astype(BF16)
    eye8 = jnp.where(lax.broadcasted_iota(jnp.int32, (8, LANES), 0)
                     == lax.broadcasted_iota(jnp.int32, (8, LANES), 1), 1.0, 0.0).astype(BF16)
    g3 = _split3(gates)
    cums = sum(jnp.dot(tril, gi, preferred_element_type=F32) for gi in g3)
    gt8 = sum(_nt(eye8, gi) for gi in g3)
    bt8 = sum(_nt(eye8, ci) for ci in _split3(cums))

    for h in range(M_HEADS):
        sl = slice(h * M_HEAD_DIM, (h + 1) * M_HEAD_DIM)
        q = qkv_ref[:, h * M_HEAD_DIM:(h + 1) * M_HEAD_DIM]
        k = qkv_ref[:, D_MODEL + h * M_HEAD_DIM:D_MODEL + (h + 1) * M_HEAD_DIM]
        v = qkv_ref[:, 2 * D_MODEL + h * M_HEAD_DIM:2 * D_MODEL + (h + 1) * M_HEAD_DIM]
        ig_col = gates[:, h:h + 1]
        ig_row = gt8[h:h + 1, :]
        b_col = cums[:, M_HEADS + h:M_HEADS + h + 1]
        b_row = bt8[M_HEADS + h:M_HEADS + h + 1, :]
        m_prev = m_ref[0, h:h + 1, 0:1]
        c_old = c_ref[0, h]
        n_old = n_ref[0, h:h + 1, :]

        dmat = jnp.where(causal, b_col - b_row + ig_row, NEG)
        m_inter = b_col + m_prev
        m_t = jnp.maximum(m_inter, jnp.max(dmat, axis=-1, keepdims=True))
        w_inter = jnp.exp(m_inter - m_t)
        a = jnp.exp(dmat - m_t) * _nt(q, k)
        num = w_inter * _nt(q, c_old.astype(BF16)) + jnp.dot(a.astype(BF16), v, preferred_element_type=F32)
        qn = jnp.sum(q.astype(F32) * n_old, axis=-1, keepdims=True)
        den = w_inter * qn + jnp.sum(a, axis=-1, keepdims=True)
        hm = num / jnp.maximum(jnp.abs(den), jnp.exp(-m_t))

        b_last = b_col[L - 1:L, :]
        g_row = b_last - b_row + ig_row
        g_col = b_last - b_col + ig_col
        m_new = jnp.maximum(b_last + m_prev, jnp.max(g_row, axis=-1, keepdims=True))
        dec = jnp.exp(b_last + m_prev - m_new)
        w_col = jnp.exp(g_col - m_new)
        vw = (v.astype(F32) * w_col).astype(BF16)
        c_ref[0, h] = dec * c_old + _tn(vw, k)
        n_ref[0, h:h + 1, :] = dec * n_old + jnp.sum(k.astype(F32) * w_col, axis=0, keepdims=True)
        m_ref[0, h:h + 1, :] = jnp.broadcast_to(m_new, (1, LANES))

        ya_ref[:, sl] = _rms(hm, nw_ref[:, sl]) * jax.nn.sigmoid(mo_ref[:, sl])


def _mlstm(qkv, gates, mo, norm_w, c0, n0, m0, *, batch, L):
    t_total = qkv.shape[0]
    nc = t_total // (batch * L)
    tok = lambda w: pl.BlockSpec((L, w), lambda b, c: (b * nc + c, 0))
    st4 = pl.BlockSpec((1, M_HEADS, M_HEAD_DIM, M_HEAD_DIM), lambda b, c: (b, 0, 0, 0))
    st3n = pl.BlockSpec((1, M_HEADS, M_HEAD_DIM), lambda b, c: (b, 0, 0))
    st3m = pl.BlockSpec((1, M_HEADS, LANES), lambda b, c: (b, 0, 0))
    return pl.pallas_call(
        functools.partial(_mlstm_kernel, L=L),
        grid=(batch, nc),
        in_specs=[tok(3 * D_MODEL), tok(LANES), tok(D_MODEL), pl.BlockSpec((1, D_MODEL), lambda b, c: (0, 0)),
                  st4, st3n, st3m],
        out_specs=[tok(D_MODEL), st4, st3n, st3m],
        out_shape=[jax.ShapeDtypeStruct((t_total, D_MODEL), F32),
                   jax.ShapeDtypeStruct(c0.shape, F32),
                   jax.ShapeDtypeStruct(n0.shape, F32),
                   jax.ShapeDtypeStruct(m0.shape, F32)],
        compiler_params=_cparams(("parallel", "arbitrary")),
        name="mlstm",
    )(qkv, gates, mo, norm_w, c0, n0, m0)


def _lambda_full(lq1_ref, lk1_ref, lq2_ref, lk2_ref):
    a = jnp.exp(jnp.sum(lq1_ref[...] * lk1_ref[...], axis=-1, keepdims=True))
    b = jnp.exp(jnp.sum(lq2_ref[...] * lk2_ref[...], axis=-1, keepdims=True))
    return a - b + LAMBDA_INIT


def _attn_prompt_kernel(qi_ref, kj_ref, q1_ref, q2_ref, k_ref, vt_ref, lq1_ref, lk1_ref, lq2_ref, lk2_ref, dn_ref,
                        o_ref, m_sc, a_sc, acc_sc, s_sc, p_sc, *, tq):
    step = pl.program_id(0)
    qi = qi_ref[step]
    kj = kj_ref[step]
    n_sub = k_ref.shape[1] // tq
    last_kj = qi // n_sub
    q_refs = (q1_ref, q2_ref)

    @pl.when(kj == 0)
    def _():
        m_sc[...] = jnp.full(m_sc.shape, NEG, F32)
        acc_sc[...] = jnp.zeros(acc_sc.shape, F32)

    def run(subs):
        n = 2 * len(subs)

        def scores(head, g):
            sub = subs[g // 2][0]
            s_sc[g] = jnp.dot(k_ref[head, sub * tq:(sub + 1) * tq, :], q_refs[g % 2][head],
                              preferred_element_type=F32)

        def softmax(head, g):
            pair = 2 * head + g % 2
            s = s_sc[g]
            if subs[g // 2][1]:
                key = lax.broadcasted_iota(jnp.int32, (tq, tq), 0)
                qry = lax.broadcasted_iota(jnp.int32, (tq, tq), 1)
                s = jnp.where(key <= qry, s, NEG)
            m_prev = m_sc[pair]
            m_new = jnp.maximum(m_prev, jnp.max(s, axis=0, keepdims=True))
            a_sc[g] = jnp.exp2(m_prev - m_new)
            p_sc[g] = jnp.exp2(s - m_new).astype(BF16)
            m_sc[pair] = m_new

        def values(head, g):
            pair = 2 * head + g % 2
            sub = subs[g // 2][0]
            pv = jnp.dot(vt_ref[head, :, sub * tq:(sub + 1) * tq], p_sc[g], preferred_element_type=F32)
            acc_sc[pair] = a_sc[g] * acc_sc[pair] + pv

        for g in range(n):
            scores(0, g)
        softmax(0, 0)

        def body(head, carry):
            for g in range(n):
                values(head, g)
                if g + 1 < n:
                    softmax(head, g + 1)
                else:
                    softmax(head + 1, 0)
                scores(head + 1, g)
            return carry

        lax.fori_loop(0, A_HEADS - 1, body, 0)
        for g in range(n):
            values(A_HEADS - 1, g)
            if g + 1 < n:
                softmax(A_HEADS - 1, g + 1)

    @pl.when(kj < last_kj)
    def _():
        run(tuple((sub, False) for sub in range(n_sub)))

    for diag in range(n_sub):
        @pl.when((kj == last_kj) & (qi % n_sub == diag))
        def _(diag=diag):
            run(tuple((sub, False) for sub in range(diag)) + ((diag, True),))

    @pl.when(kj == last_kj)
    def _():
        lam = _lambda_full(lq1_ref, lk1_ref, lq2_ref, lk2_ref)
        for h in range(A_HEADS):
            a1 = acc_sc[2 * h]
            a2 = acc_sc[2 * h + 1]
            ot = (a1[0:A_VDIM] / a1[A_VDIM:A_VDIM + 1]
                  - lam * (a2[0:A_VDIM] / a2[A_VDIM:A_VDIM + 1]))
            o_ref[:, h * LANES:(h + 1) * LANES] = _rms(ot.T, dn_ref[...]) * (1.0 - LAMBDA_INIT)


def _attn_prompt(q1t, q2t, kh, vt, lams, diff_norm, *, tq, n_sub):
    t = kh.shape[1]
    nq = t // tq
    tk = n_sub * tq
    assert t % tk == 0
    steps = [(qi, kj) for qi in range(nq) for kj in range(qi // n_sub + 1)]
    qi_of = jnp.asarray([s[0] for s in steps], jnp.int32)
    kj_of = jnp.asarray([s[1] for s in steps], jnp.int32)
    qspec = pl.BlockSpec((A_HEADS, A_VDIM, tq), lambda s, qi, kj: (0, 0, qi[s]))
    kspec = pl.BlockSpec((A_HEADS, tk, A_VDIM), lambda s, qi, kj: (0, kj[s], 0))
    vspec = pl.BlockSpec((A_HEADS, VT_ROWS, tk), lambda s, qi, kj: (0, 0, kj[s]))
    small = lambda n: pl.BlockSpec((1, n), lambda s, qi, kj: (0, 0))
    grid_spec = pltpu.PrefetchScalarGridSpec(
        num_scalar_prefetch=2,
        grid=(len(steps),),
        in_specs=[qspec, qspec, kspec, vspec] + [small(A_HEAD_DIM)] * 4 + [small(A_VDIM)],
        out_specs=pl.BlockSpec((tq, D_MODEL), lambda s, qi, kj: (qi[s], 0)),
        scratch_shapes=[pltpu.VMEM((2 * A_HEADS, 1, tq), F32),
                        pltpu.VMEM((2 * n_sub, 1, tq), F32),
                        pltpu.VMEM((2 * A_HEADS, VT_ROWS, tq), F32),
                        pltpu.VMEM((2 * n_sub, tq, tq), F32), pltpu.VMEM((2 * n_sub, tq, tq), BF16)],
    )
    return pl.pallas_call(
        functools.partial(_attn_prompt_kernel, tq=tq),
        grid_spec=grid_spec,
        out_shape=jax.ShapeDtypeStruct((t, D_MODEL), F32),
        compiler_params=_cparams(("arbitrary",)),
        name="diff_attn_prompt",
    )(qi_of, kj_of, q1t, q2t, kh, vt, *lams, diff_norm)


def _attn_sample_kernel(pt_ref, q_ref, kn_ref, vn_ref, lq1_ref, lk1_ref, lq2_ref, lk2_ref, dn_ref, *rest,
                        ppb, n_new, page):
    k_refs = rest[:ppb]
    v_refs = rest[ppb:2 * ppb]
    o_ref, m_sc, l_sc, acc_sc = rest[2 * ppb:]
    j = pl.program_id(1)
    rows = q_ref.shape[2]

    @pl.when(j == 0)
    def _():
        m_sc[...] = jnp.full(m_sc.shape, NEG, F32)
        l_sc[...] = jnp.zeros(l_sc.shape, F32)
        acc_sc[...] = jnp.zeros(acc_sc.shape, F32)

    def update(scores, values):
        s = jnp.concatenate([scores(h) for h in range(A_HEADS)], axis=0)
        m_prev = m_sc[...]
        m_new = jnp.maximum(m_prev, jnp.max(s, axis=-1, keepdims=True))
        alpha = jnp.exp(m_prev - m_new)
        p = jnp.exp(s - m_new)
        l_sc[...] = alpha * l_sc[...] + jnp.sum(p, axis=-1, keepdims=True)
        pb = p.astype(BF16)
        pv = jnp.concatenate([jnp.dot(pb[h * rows:(h + 1) * rows], values(h), preferred_element_type=F32)
                              for h in range(A_HEADS)], axis=0)
        acc_sc[...] = alpha * acc_sc[...] + pv
        m_sc[...] = m_new

    def head_rows(refs, h):
        parts = [r[0, pl.ds(h, page, stride=A_HEADS), :] for r in refs]
        return jnp.concatenate(parts, axis=0).astype(BF16)

    update(lambda h: _nt(q_ref[0, h], head_rows(k_refs, h)), lambda h: head_rows(v_refs, h))

    @pl.when(j == pl.num_programs(1) - 1)
    def _():
        lam = _lambda_full(lq1_ref, lk1_ref, lq2_ref, lk2_ref)
        r = lax.broadcasted_iota(jnp.int32, (rows, rows), 0)
        c = lax.broadcasted_iota(jnp.int32, (rows, rows), 1)
        keep = (c <= r % n_new) & (c < n_new)
        update(lambda h: jnp.where(keep, _nt(q_ref[0, h], kn_ref[0, h]), NEG), lambda h: vn_ref[0, h])
        o = acc_sc[...] / l_sc[...]
        for h in range(A_HEADS):
            od = o[h * rows:h * rows + n_new, :] - lam * o[h * rows + n_new:h * rows + 2 * n_new, :]
            o_ref[0, :, h * LANES:(h + 1) * LANES] = _rms(od, dn_ref[...]) * (1.0 - LAMBDA_INIT)


def _attn_sample(q_rows, k_new, v_new, cache_k, cache_v, page_table, lams, diff_norm, *, ppb, n_new, page_base):
    bsz, n_pages = page_table.shape
    page = cache_k.shape[1] // A_HEADS
    rows = q_rows.shape[2]
    seq = lambda: pl.BlockSpec((1, A_HEADS, rows, A_VDIM), lambda b, j, pt: (b, 0, 0, 0))
    small = lambda n: pl.BlockSpec((1, n), lambda b, j, pt: (0, 0))

    def page_spec(i):
        return pl.BlockSpec((1, page * A_HEADS, A_VDIM), lambda b, j, pt: (page_base + pt[b, j * ppb + i], 0, 0))

    grid_spec = pltpu.PrefetchScalarGridSpec(
        num_scalar_prefetch=1,
        grid=(bsz, n_pages // ppb),
        in_specs=[seq(), seq(), seq()] + [small(A_HEAD_DIM)] * 4 + [small(A_VDIM)]
                 + [page_spec(i) for i in range(ppb)] + [page_spec(i) for i in range(ppb)],
        out_specs=pl.BlockSpec((1, n_new, D_MODEL), lambda b, j, pt: (b, 0, 0)),
        scratch_shapes=[pltpu.VMEM((A_HEADS * rows, 1), F32),
                        pltpu.VMEM((A_HEADS * rows, 1), F32),
                        pltpu.VMEM((A_HEADS * rows, A_VDIM), F32)],
    )
    return pl.pallas_call(
        functools.partial(_attn_sample_kernel, ppb=ppb, n_new=n_new, page=page),
        grid_spec=grid_spec,
        out_shape=jax.ShapeDtypeStruct((bsz, n_new, D_MODEL), F32),
        compiler_params=_cparams(("parallel", "arbitrary")),
        name="diff_attn_sample",
    )(page_table, q_rows, k_new, v_new, *lams, diff_norm, *([cache_k] * ppb), *([cache_v] * ppb))


def _postmix_kernel(x_ref, ya_ref, yb_ref, gab_ref, g1_ref, sc2_ref, sh2_ref, gpost_ref, gpre_ref, wo_ref, wq_ref,
                    x1_ref, h2_ref, qp_ref):
    merged = (jax.nn.sigmoid(gab_ref[:, 0:D_MODEL]) * ya_ref[...]
              + jax.nn.sigmoid(gab_ref[:, D_MODEL:2 * D_MODEL]) * yb_ref[...])
    o = jnp.dot(merged.astype(BF16), wo_ref[...], preferred_element_type=F32)
    x1 = x_ref[...] + g1_ref[...] * _rms(o, gpost_ref[...])
    x1_ref[...] = x1
    h2 = (_rms(x1, gpre_ref[...]) * (1.0 + sc2_ref[...]) + sh2_ref[...]).astype(BF16)
    h2_ref[...] = h2
    qp_ref[...] = jnp.dot(h2, wq_ref[...], preferred_element_type=F32).astype(BF16)


NOT_TOP = 255.0


def _top_ranked(s, n):
    vals = []
    work = s
    rank = jnp.full(s.shape, NOT_TOP, F32)
    for i in range(n):
        mx = jnp.max(work, axis=0, keepdims=True)
        hit = work == mx
        rank = jnp.where(hit, float(i), rank)
        work = jnp.where(hit, NEG, work)
        vals.append(mx)
    return vals, rank


def _peer_score_kernel(qp_ref, k1_ref, k2_ref, c1_ref, e1_ref, r2_ref, e2_ref, *, tt):
    ab = N_KEYS // 8
    jrow = lax.broadcasted_iota(jnp.int32, (8, tt), 0)
    for h in range(PEER_HEADS):
        base = h * 2 * N_KEYS
        s1 = _nt(k1_ref[h], qp_ref[:, base:base + N_KEYS])
        s2 = _nt(k2_ref[h], qp_ref[:, base + N_KEYS:base + 2 * N_KEYS])
        t1, _ = _top_ranked(s1, PEER_TOPK)
        t2, rank2 = _top_ranked(s2, PEER_TOPK)
        t2a = jnp.concatenate(t2[:8], axis=0)
        t2b = jnp.concatenate(t2[8:], axis=0)
        slabs = [t1[0] + t2a, t1[0] + t2b]
        for i in range(1, PEER_TOPK):
            slabs.append(jnp.where(jrow < PEER_TOPK // (i + 1), t1[i] + t2a, NEG))
        cand = jnp.concatenate(slabs, axis=0)
        tau = _top_ranked(cand, PEER_TOPK)[0][-1]
        top = t1[0] + t2[0]
        sel = cand >= tau
        z = jnp.sum(jnp.where(sel, jnp.exp(cand - top), 0.0), axis=0, keepdims=True)
        ones = jnp.where(sel, 1.0, 0.0)
        counts = [jnp.sum(ones[0:16], axis=0, keepdims=True)]
        for i in range(1, PEER_TOPK):
            counts.append(jnp.sum(ones[8 + 8 * i:16 + 8 * i], axis=0, keepdims=True))
        c1 = jnp.zeros(s1.shape, F32)
        for i in range(PEER_TOPK):
            c1 = jnp.where(s1 == t1[i], counts[i], c1)
        c1_ref[:, h] = c1.reshape(ab, 8, tt)
        e1_ref[:, h] = (jnp.exp(s1 - t1[0]) * (0.5 / z)).reshape(ab, 8, tt)
        r2_ref[h] = rank2.astype(BF16)
        e2_ref[h] = jnp.exp(s2 - t2[0]).astype(BF16)


def _postmix_scores_kernel(x_ref, ya_ref, yb_ref, gab_ref, g1_ref, sc2_ref, sh2_ref, gpost_ref, gpre_ref, wo_ref,
                           wq_ref, k1_ref, k2_ref, x1_ref, h2_ref, c1_ref, e1_ref, r2_ref, e2_ref, qp_sc, *, tt):
    _postmix_kernel(x_ref, ya_ref, yb_ref, gab_ref, g1_ref, sc2_ref, sh2_ref, gpost_ref, gpre_ref, wo_ref, wq_ref,
                    x1_ref, h2_ref, qp_sc)
    _peer_score_kernel(qp_sc, k1_ref, k2_ref, c1_ref, e1_ref, r2_ref, e2_ref, tt=tt)


def _postmix_scores(x, ya, yb, gab, gate1, scale2, shift2, g_post1, g_pre2, w_out, w_q, k1, k2, *, tt):
    t = x.shape[0]
    nq = w_q.shape[1]
    ab = N_KEYS // 8
    if gate1.shape[0] == 1:
        mod_spec = pl.BlockSpec((1, D_MODEL), lambda i: (0, 0))
    else:
        mod_spec = pl.BlockSpec((tt, D_MODEL), lambda i: (i, 0))
    const = lambda shape: pl.BlockSpec(shape, lambda i: tuple(0 for _ in shape))
    tok = lambda w: pl.BlockSpec((tt, w), lambda i: (i, 0))
    a_spec = pl.BlockSpec((ab, PEER_HEADS, 8, tt), lambda i: (0, 0, 0, i))
    b_spec = pl.BlockSpec((PEER_HEADS, N_KEYS, tt), lambda i: (0, 0, i))
    a_shape = jax.ShapeDtypeStruct((ab, PEER_HEADS, 8, t), F32)
    b_shape = jax.ShapeDtypeStruct((PEER_HEADS, N_KEYS, t), BF16)
    return pl.pallas_call(
        functools.partial(_postmix_scores_kernel, tt=tt),
        grid=(t // tt,),
        in_specs=[tok(D_MODEL), tok(D_MODEL), tok(D_MODEL), tok(2 * D_MODEL), mod_spec, mod_spec, mod_spec,
                  const((1, D_MODEL)), const((1, D_MODEL)), const((D_MODEL, D_MODEL)), const((D_MODEL, nq)),
                  const((PEER_HEADS, N_KEYS, N_KEYS)), const((PEER_HEADS, N_KEYS, N_KEYS))],
        out_specs=[tok(D_MODEL), tok(D_MODEL), a_spec, a_spec, b_spec, b_spec],
        out_shape=[jax.ShapeDtypeStruct((t, D_MODEL), F32), jax.ShapeDtypeStruct((t, D_MODEL), BF16),
                   a_shape, a_shape, b_shape, b_shape],
        scratch_shapes=[pltpu.VMEM((tt, nq), BF16)],
        compiler_params=_cparams(("parallel",)),
        name="post_mix_scores",
    )(x, ya, yb, gab, gate1, scale2, shift2, g_post1, g_pre2, w_out, w_q, k1, k2)


def _gelu_x2(x):
    return x * (1.0 + lax.erf(x * (2.0 ** -0.5)))


def _peer_dense_kernel(h2_ref, u_ref, v_ref, c1_ref, e1_ref, r2_ref, e2_ref, x1_ref, g2_ref, gpost_ref,
                       y_ref, acc_sc, a_sc, *, a_per):
    j = pl.program_id(1)

    @pl.when(j == 0)
    def _():
        acc_sc[...] = jnp.zeros(acc_sc.shape, F32)

    zt = _nt(u_ref[...], h2_ref[...])
    zero = jnp.zeros((), BF16)
    tt = h2_ref.shape[0]

    def key_rows(ref, al, h):
        row = ref[al // 8, h, al % 8:al % 8 + 1, :]
        tile = jnp.broadcast_to(row, (16, tt)).astype(BF16)
        return jnp.tile(tile, (N_KEYS // 16, 1))

    for al in range(a_per):
        w = None
        for h in range(PEER_HEADS):
            term = jnp.where(r2_ref[h] < key_rows(c1_ref, al, h), e2_ref[h] * key_rows(e1_ref, al, h), zero)
            w = term if w is None else w + term
        rows = slice(al * N_KEYS, (al + 1) * N_KEYS)
        a_sc[rows, :] = w * _gelu_x2(zt[rows, :]).astype(BF16)
    acc_sc[...] += _tn(v_ref[...], a_sc[...])

    @pl.when(j == pl.num_programs(1) - 1)
    def _():
        f = acc_sc[...].T
        y_ref[...] = x1_ref[...] + g2_ref[...] * _rms(f, gpost_ref[...])


def _peer_dense(h2, u_b, v_b, c1, e1, r2, e2, x1, gate2, g_post2, *, tt, a_per):
    t = h2.shape[0]
    n_exp = u_b.shape[0]
    eb = a_per * N_KEYS
    if gate2.shape[0] == 1:
        mod_spec = pl.BlockSpec((1, D_MODEL), lambda i, j: (0, 0))
    else:
        mod_spec = pl.BlockSpec((tt, D_MODEL), lambda i, j: (i, 0))
    a_spec = pl.BlockSpec((a_per // 8, PEER_HEADS, 8, tt), lambda i, j: (j, 0, 0, i))
    b_spec = pl.BlockSpec((PEER_HEADS, N_KEYS, tt), lambda i, j: (0, 0, i))
    return pl.pallas_call(
        functools.partial(_peer_dense_kernel, a_per=a_per),
        grid=(t // tt, n_exp // eb),
        in_specs=[pl.BlockSpec((tt, D_MODEL), lambda i, j: (i, 0)),
                  pl.BlockSpec((eb, D_MODEL), lambda i, j: (j, 0)),
                  pl.BlockSpec((eb, D_MODEL), lambda i, j: (j, 0)),
                  a_spec, a_spec, b_spec, b_spec,
                  pl.BlockSpec((tt, D_MODEL), lambda i, j: (i, 0)),
                  mod_spec,
                  pl.BlockSpec((1, D_MODEL), lambda i, j: (0, 0))],
        out_specs=pl.BlockSpec((tt, D_MODEL), lambda i, j: (i, 0)),
        out_shape=jax.ShapeDtypeStruct((t, D_MODEL), F32),
        scratch_shapes=[pltpu.VMEM((D_MODEL, tt), F32), pltpu.VMEM((eb, tt), BF16)],
        compiler_params=_cparams(("parallel", "arbitrary")),
        name="peer_dense",
    )(h2, u_b, v_b, c1, e1, r2, e2, x1, gate2, g_post2)


def _block(t, pref):
    return pref if t % pref == 0 else t


def _layer(x, mods, weights, *, pos0, period, run_mlstm, attend, head_major):
    shift1, scale1, gate1, shift2, scale2, gate2 = mods
    w = weights
    t = x.shape[0]
    tb = _block(t, 256)
    qkv, mo, gab, gates, q1, q2, kf, kb, vf, vb = _inproj(
        x, scale1, shift1, w["g_pre1"], w["w_main"], w["w_if"], w["b_if128"], w["inv128"],
        pos0=pos0, period=period, tb=tb, head_major=head_major)

    ya, c_new, n_new, m_new = run_mlstm(qkv, gates, mo)
    yb = attend(q1, q2, kb, vb)

    x1, h2, c1, e1, r2, e2 = _postmix_scores(x, ya, yb, gab, gate1, scale2, shift2, w["g_post1"], w["g_pre2"],
                                             w["w_out"], w["w_q"], w["k1"], w["k2"], tt=_block(t, 256))
    ttd = _block(t, 1024)
    y = _peer_dense(h2, w["u"], w["v"], c1, e1, r2, e2, x1, gate2, w["g_post2"], tt=ttd, a_per=8)
    return y, kf, vf, c_new, n_new, m_new


def _hybrid_step(x_prompt, x_sample, cache_k, cache_v, state_C, state_n, state_m, page_table, c_prompt, c_sample,
                 w_ada, b_ada, g_pre1, g_post1, g_pre2, g_post2, w_in, b_if, mlstm_norm,
                 lambda_q1, lambda_k1, lambda_q2, lambda_k2, diff_norm, w_out,
                 peer_wq, peer_k1, peer_k2, peer_u, peer_v, *, m_chunk=128, tq=512, attn_sub=4, ppb=16):
    l = 0
    bp, sp, _ = x_prompt.shape
    bs, ts, _ = x_sample.shape
    assert bp == 1
    row = lambda a: a.reshape(1, -1)

    wi = w_in[l]
    n_if = 2 * M_HEADS
    w_main = jnp.concatenate([wi[:, :4 * D_MODEL], wi[:, 4 * D_MODEL + n_if:]], axis=1).astype(BF16)
    w_if = jnp.pad(wi[:, 4 * D_MODEL:4 * D_MODEL + n_if], ((0, 0), (0, LANES - n_if))).astype(BF16)
    b_if128 = jnp.pad(b_if[l], (0, LANES - n_if)).reshape(1, LANES)
    inv = ROPE_THETA ** (-jnp.arange(0, A_HEAD_DIM, 2, dtype=F32) / A_HEAD_DIM)
    inv128 = jnp.tile(inv, LANES // inv.shape[0]).reshape(1, LANES)
    weights = dict(
        g_pre1=row(g_pre1[l]), g_post1=row(g_post1[l]), g_pre2=row(g_pre2[l]), g_post2=row(g_post2[l]),
        w_main=w_main, w_if=w_if, b_if128=b_if128, inv128=inv128,
        w_out=w_out[l].astype(BF16), w_q=peer_wq[l].astype(BF16),
        k1=peer_k1[l].astype(BF16), k2=peer_k2[l].astype(BF16),
        u=peer_u[l].astype(BF16), v=peer_v[l].astype(BF16))
    lams = [row(a[l]) for a in (lambda_q1, lambda_k1, lambda_q2, lambda_k2)]
    dn = row(diff_norm[l])
    nw = row(mlstm_norm[l])

    n_c = bp + bs
    c_all = jnp.pad(jnp.concatenate([c_prompt, c_sample], axis=0), ((0, (-n_c) % 8), (0, 0)))
    mod = _ada(c_all, w_ada[l], b_ada[l])
    mods_p = [mod[0:1, i * D_MODEL:(i + 1) * D_MODEL] for i in range(6)]
    mods_s = [jnp.repeat(mod[bp:n_c, i * D_MODEL:(i + 1) * D_MODEL], ts, axis=0) for i in range(6)]

    def run_mlstm_p(qkv, gates, mo):
        c0 = jnp.zeros((bp, M_HEADS, M_HEAD_DIM, M_HEAD_DIM), F32)
        n0 = jnp.zeros((bp, M_HEADS, M_HEAD_DIM), F32)
        m0 = jnp.zeros((bp, M_HEADS, LANES), F32)
        return _mlstm(qkv, gates, mo, nw, c0, n0, m0, batch=bp, L=math.gcd(sp, m_chunk))

    def attend_p(q1t, q2t, kh, vt):
        tq_p = _block(sp, tq)
        return _attn_prompt(q1t, q2t, kh, vt, lams, dn, tq=tq_p, n_sub=math.gcd(sp // tq_p, attn_sub))

    yp, kp, vp, cp, n_p, mp = _layer(
        x_prompt.reshape(sp, D_MODEL), mods_p, weights, pos0=0, period=sp,
        run_mlstm=run_mlstm_p, attend=attend_p, head_major=True)

    lpad = 16
    past_len = page_table.shape[1] * cache_k.shape[2]

    def pad_tokens(a, fill_row=None):
        a3 = a.reshape(bs, ts, a.shape[-1])
        if fill_row is None:
            out = jnp.pad(a3, ((0, 0), (0, lpad - ts), (0, 0)))
        else:
            fill = jnp.broadcast_to(fill_row.astype(a.dtype), (bs, lpad - ts, a.shape[-1]))
            out = jnp.concatenate([a3, fill], axis=1)
        return out.reshape(bs * lpad, a.shape[-1])

    def run_mlstm_s(qkv, gates, mo):
        lane = jnp.arange(LANES)
        gate_fill = jnp.where(lane < M_HEADS, NEG, 0.0).astype(F32)
        m0 = jnp.broadcast_to(state_m[l][:, :, None], (bs, M_HEADS, LANES))
        ya, c_new, n_new, m_new = _mlstm(pad_tokens(qkv), pad_tokens(gates, gate_fill), pad_tokens(mo), nw,
                                         state_C[l], state_n[l], m0, batch=bs, L=lpad)
        ya = ya.reshape(bs, lpad, D_MODEL)[:, :ts].reshape(bs * ts, D_MODEL)
        return ya, c_new, n_new, m_new

    def attend_s(q1, q2, kb, vb):
        heads = lambda a: a.reshape(bs, ts, A_HEADS, A_VDIM).transpose(0, 2, 1, 3)
        zeros = lambda n: jnp.zeros((bs, A_HEADS, n, A_VDIM), BF16)
        q_rows = jnp.concatenate([heads(q1), heads(q2), zeros(lpad - 2 * ts)], axis=2)
        k_new = jnp.concatenate([heads(kb), zeros(lpad - ts)], axis=2)
        v_new = jnp.concatenate([heads(vb), zeros(lpad - ts)], axis=2)
        depth, n_pool, page = cache_k.shape[0], cache_k.shape[1], cache_k.shape[2]
        ck = cache_k.reshape(depth * n_pool, page * A_HEADS, A_VDIM)
        cv = cache_v.reshape(depth * n_pool, page * A_HEADS, A_VDIM)
        o = _attn_sample(q_rows, k_new, v_new, ck, cv, page_table, lams, dn,
                         ppb=math.gcd(page_table.shape[1], ppb), n_new=ts, page_base=l * n_pool)
        return o.reshape(bs * ts, D_MODEL)

    ys, k_s, v_s, cs, n_s, ms = _layer(
        x_sample.reshape(bs * ts, D_MODEL), mods_s, weights, pos0=past_len, period=ts,
        run_mlstm=run_mlstm_s, attend=attend_s, head_major=False)

    return (yp.reshape(bp, sp, D_MODEL), ys.reshape(bs, ts, D_MODEL),
            kp.reshape(1, bp, sp, A_HEADS, A_VDIM), vp.reshape(1, bp, sp, A_HEADS, A_VDIM),
            cp[None], n_p[None], mp[None, :, :, 0],
            k_s.reshape(1, bs, ts, A_HEADS, A_VDIM), v_s.reshape(1, bs, ts, A_HEADS, A_VDIM),
            cs[None], n_s[None], ms[None, :, :, 0])


def kernel(x_prompt, x_sample, cache_k, cache_v, state_C, state_n, state_m, page_table, c_prompt, c_sample, w_ada, b_ada, g_pre1, g_post1, g_pre2, g_post2, w_in, b_if, mlstm_norm, lambda_q1, lambda_k1, lambda_q2, lambda_k2, diff_norm, w_out, peer_wq, peer_k1, peer_k2, peer_u, peer_v):
    return _hybrid_step(x_prompt, x_sample, cache_k, cache_v, state_C, state_n, state_m, page_table, c_prompt, c_sample,
                        w_ada, b_ada, g_pre1, g_post1, g_pre2, g_post2, w_in, b_if, mlstm_norm,
                        lambda_q1, lambda_k1, lambda_q2, lambda_k2, diff_norm, w_out,
                        peer_wq, peer_k1, peer_k2, peer_u, peer_v)
```
